```python
import math
import jax, jax.numpy as jnp
from jax import lax
import numpy as np

D_MODEL = 2048
BATCH = 2
SEQ = 4096
DEPTH = 1

CHUNK = 64
SB_HEADS = 8
SB_HEAD_DIM = 128
SB_WIDTH = SB_HEADS * SB_HEAD_DIM
SB_QBLOCK = 128
HG_HEADS = 8
HG_KEY_DIM = 128
HG_VAL_DIM = 128
HG_WIDTH = HG_HEADS * HG_KEY_DIM
HG_VWIDTH = HG_HEADS * HG_VAL_DIM
RMS_EPS = 1e-6
PEER_HEADS = 8
PEER_HALF = 128
PEER_QDIM = 2 * PEER_HALF
N_KEYS = 128
N_EXPERTS = N_KEYS * N_KEYS
PEER_TOPK = 16
PEER_TOKBLOCK = 128
DEEPNORM_ALPHA = (2.0 * DEPTH) ** 0.25
DEEPNORM_BETA = (8.0 * DEPTH) ** -0.25
LN_EPS = 1e-5
IN_SIZES = (SB_WIDTH, SB_WIDTH, SB_WIDTH,
            HG_WIDTH, HG_WIDTH, HG_VWIDTH, HG_VWIDTH,
            D_MODEL, D_MODEL)
IN_OFFSETS = tuple(int(v) for v in np.cumsum(IN_SIZES)[:-1])
D_IN = int(sum(IN_SIZES))

kernel_name = "hybrid_sb_hgrn2_peer_deepnorm_block"


def layer_norm(x, g, b):
    xf = x.astype(jnp.float32)
    mu = jnp.mean(xf, axis=-1, keepdims=True)
    var = jnp.mean(jnp.square(xf - mu), axis=-1, keepdims=True)
    y = (xf - mu) * lax.rsqrt(var + LN_EPS) * g.astype(jnp.float32) + b.astype(jnp.float32)
    return y.astype(x.dtype)


def split_heads(t, n_heads):
    bsz, seq, _ = t.shape
    return t.reshape(bsz, seq, n_heads, -1).transpose(0, 2, 1, 3)


def merge_heads(t):
    bsz, nh, seq, d = t.shape
    return t.transpose(0, 2, 1, 3).reshape(bsz, seq, nh * d)


def stick_breaking_attention(q, k, v):
    seq = q.shape[2]
    scale = 1.0 / math.sqrt(q.shape[-1])
    outs = []
    for blk in range(seq // SB_QBLOCK):
        start = blk * SB_QBLOCK
        end = start + SB_QBLOCK
        z = jnp.einsum('bhqd,bhkd->bhqk', q[:, :, start:end], k[:, :, :end]) * scale
        t_pos = start + jnp.arange(SB_QBLOCK)
        s_pos = jnp.arange(end)
        mask = s_pos[None, :] < t_pos[:, None]
        log_1mb = jnp.where(mask, jax.nn.log_sigmoid(-z), 0.0)
        tail = lax.cumsum(log_1mb, axis=3, reverse=True) - log_1mb
        w = jnp.where(mask, jnp.exp(jax.nn.log_sigmoid(z) + tail), 0.0)
        outs.append(jnp.einsum('bhqk,bhkd->bhqd', w, v[:, :, :end]))
    return jnp.concatenate(outs, axis=2)


def hgrn2_chunk_scan(q, k, log_f, v):
    bsz, nh, seq, dk = q.shape
    dv = v.shape[-1]
    nc = seq // CHUNK

    def to_chunks(t):
        return t.reshape(bsz, nh, nc, CHUNK, t.shape[-1]).transpose(2, 0, 1, 3, 4)

    causal = jnp.tril(jnp.ones((CHUNK, CHUNK), dtype=bool))

    def step(state, inp):
        qc, kc, lfc, vc = inp
        b = jnp.cumsum(lfc, axis=2)
        diff = b[:, :, :, None, :] - b[:, :, None, :, :]
        decay = jnp.exp(jnp.where(causal[:, :, None], diff, -jnp.inf))
        scores = jnp.einsum('bhtd,bhsd,bhtsd->bhts', qc, kc, decay)
        o = (jnp.einsum('bhts,bhse->bhte', scores, vc)
             + jnp.einsum('bhtd,bhde->bhte', qc * jnp.exp(b), state))
        b_last = b[:, :, -1:, :]
        new_state = (jnp.exp(b_last[:, :, 0, :])[..., None] * state
                     + jnp.einsum('bhsd,bhse->bhde', kc * jnp.exp(b_last - b), vc))
        return new_state, o

    init = jnp.zeros((bsz, nh, dk, dv), jnp.float32)
    _, o = lax.scan(step, init, (to_chunks(q), to_chunks(k), to_chunks(log_f), to_chunks(v)))
    return o.transpose(1, 2, 0, 3, 4).reshape(bsz, nh, seq, dv)


def hybrid_mixer(h, w_in, lower_bound, hg_norm_g, w_branch_a, w_branch_b, w_out):
    f32 = jnp.float32
    proj = h @ w_in
    sb_q, sb_k, sb_v, hg_q, hg_f, hg_i, hg_g, gate_a, gate_b = jnp.split(proj, IN_OFFSETS, axis=-1)
    a_out = stick_breaking_attention(split_heads(sb_q, SB_HEADS).astype(f32),
                                     split_heads(sb_k, SB_HEADS).astype(f32),
                                     split_heads(sb_v, SB_HEADS).astype(f32))
    a_out = merge_heads(a_out).astype(h.dtype)
    lb = lower_bound.astype(f32).reshape(HG_HEADS, 1, HG_KEY_DIM)
    fgate = lb + (1.0 - lb) * jax.nn.sigmoid(split_heads(hg_f, HG_HEADS).astype(f32))
    o = hgrn2_chunk_scan(split_heads(hg_q, HG_HEADS).astype(f32), 1.0 - fgate, jnp.log(fgate),
                         split_heads(hg_i, HG_HEADS).astype(f32))
    o = o * lax.rsqrt(jnp.mean(jnp.square(o), axis=-1, keepdims=True) + RMS_EPS)
    b_out = (merge_heads(o) * hg_norm_g.astype(f32)
             * jax.nn.silu(hg_g.astype(f32))).astype(h.dtype)
    merged = jax.nn.sigmoid(gate_a) * (a_out @ w_branch_a) + jax.nn.sigmoid(gate_b) * (b_out @ w_branch_b)
    return merged @ w_out


def peer_ffn(h, w_q, sub_keys, u_tab, v_tab):
    bsz, seq, d = h.shape
    n_tok = bsz * seq
    hf = h.reshape(n_tok, d)
    q = (hf @ w_q).reshape(n_tok, PEER_HEADS, 2, PEER_HALF).astype(jnp.float32)
    half_scores = jnp.einsum('thpd,hpnd->thpn', q, sub_keys.astype(jnp.float32))
    s_top, i_top = lax.top_k(half_scores, PEER_TOPK)
    cand = (s_top[:, :, 0, :, None] + s_top[:, :, 1, None, :]).reshape(n_tok, PEER_HEADS, -1)
    cand_idx = (i_top[:, :, 0, :, None] * N_KEYS + i_top[:, :, 1, None, :]).reshape(n_tok, PEER_HEADS, -1)
    best, pos = lax.top_k(cand, PEER_TOPK)
    expert_idx = jnp.take_along_axis(cand_idx, pos, axis=-1)
    gates = jax.nn.softmax(best, axis=-1).astype(h.dtype)
    n_blk = n_tok // PEER_TOKBLOCK

    def block(args):
        xb, ib, gb = args
        act = jnp.einsum('td,thkd->thk', xb, u_tab[ib])
        w = gb * jax.nn.gelu(act, approximate=False)
        return jnp.einsum('thk,thkd->td', w, v_tab[ib])

    out = lax.map(block, (hf.reshape(n_blk, PEER_TOKBLOCK, d),
                          expert_idx.reshape(n_blk, PEER_TOKBLOCK, PEER_HEADS, PEER_TOPK),
                          gates.reshape(n_blk, PEER_TOKBLOCK, PEER_HEADS, PEER_TOPK)))
    return out.reshape(bsz, seq, d)


def setup_inputs(seed: int = 0) -> dict:
    key = jax.random.key(seed)
    ks = jax.random.split(key, 20)

    def nrm(k, shape, scale):
        return jax.random.normal(k, shape, jnp.float32) * scale

    L = DEPTH
    return {
        "x": nrm(ks[0], (BATCH, SEQ, D_MODEL), 1.0),
        "c": nrm(ks[1], (BATCH, D_MODEL), 1.0),
        "w_ada": nrm(ks[2], (L, D_MODEL, 6 * D_MODEL), 0.5 * D_MODEL ** -0.5),
        "b_ada": nrm(ks[3], (L, 6 * D_MODEL), 0.01),
        "w_in": nrm(ks[4], (L, D_MODEL, D_IN), D_MODEL ** -0.5),
        "hg_lb_logits": nrm(ks[5], (DEPTH + 1, HG_WIDTH), 0.1),
        "hg_norm_g": 1.0 + nrm(ks[6], (L, HG_VWIDTH), 0.02),
        "w_branch_a": nrm(ks[7], (L, SB_WIDTH, D_MODEL), SB_WIDTH ** -0.5),
        "w_branch_b": nrm(ks[8], (L, HG_VWIDTH, D_MODEL), HG_VWIDTH ** -0.5),
        "w_out": nrm(ks[9], (L, D_MODEL, D_MODEL), DEEPNORM_BETA * D_MODEL ** -0.5),
        "ln1_g": 1.0 + nrm(ks[10], (L, D_MODEL), 0.02),
        "ln1_b": nrm(ks[11], (L, D_MODEL), 0.02),
        "peer_wq": nrm(ks[12], (L, D_MODEL, PEER_HEADS * PEER_QDIM), D_MODEL ** -0.5),
        "peer_subkeys": nrm(ks[13], (L, PEER_HEADS, 2, N_KEYS, PEER_HALF), PEER_HALF ** -0.5),
        "peer_u": nrm(ks[14], (L, N_EXPERTS, D_MODEL), D_MODEL ** -0.5),
        "peer_v": nrm(ks[15], (L, N_EXPERTS, D_MODEL), DEEPNORM_BETA),
        "ln2_g": 1.0 + nrm(ks[16], (L, D_MODEL), 0.02),
        "ln2_b": nrm(ks[17], (L, D_MODEL), 0.02),
    }


def reference(x, c, w_ada, b_ada, w_in, hg_lb_logits, hg_norm_g, w_branch_a, w_branch_b, w_out,
              ln1_g, ln1_b, peer_wq, peer_subkeys, peer_u, peer_v, ln2_g, ln2_b):
    lower_bounds = jnp.cumsum(jax.nn.softmax(hg_lb_logits.astype(jnp.float32), axis=0), axis=0)
    cond = jax.nn.silu(c)
    for layer in range(DEPTH):
        mod = cond @ w_ada[layer] + b_ada[layer]
        shift1, scale1, gate1, shift2, scale2, gate2 = [m[:, None, :] for m in jnp.split(mod, 6, axis=-1)]
        h = x * (1.0 + scale1) + shift1
        y = hybrid_mixer(h, w_in[layer], lower_bounds[layer], hg_norm_g[layer],
                         w_branch_a[layer], w_branch_b[layer], w_out[layer])
        x = layer_norm(DEEPNORM_ALPHA * x + (1.0 + gate1) * y, ln1_g[layer], ln1_b[layer])
        h = x * (1.0 + scale2) + shift2
        y = peer_ffn(h, peer_wq[layer], peer_subkeys[layer], peer_u[layer], peer_v[layer])
        x = layer_norm(DEEPNORM_ALPHA * x + (1.0 + gate2) * y, ln2_g[layer], ln2_b[layer])
    return x
```

```python
import functools
import math

import jax
import jax.numpy as jnp
from jax import lax
from jax.experimental import pallas as pl
from jax.experimental.pallas import tpu as pltpu

F32 = jnp.float32
BF16 = jnp.bfloat16

LANES = 128
SB_HEADS = 8
HG_HEADS = 8
HG_CHUNK = 64
PEER_HEADS = 8
PEER_TOPK = 16
N_KEYS = 128
RMS_EPS = 1e-6
LN_EPS = 1e-5
MIB = 1024 * 1024

_NT = (((1,), (1,)), ((), ()))
_TN = (((0,), (0,)), ((), ()))


def _params(semantics, vmem_mib):
    return pltpu.CompilerParams(dimension_semantics=semantics, vmem_limit_bytes=vmem_mib * MIB)


def _ada_kernel(c_ref, w_ref, b_ref, o_ref):
    c = c_ref[...]
    cond = c * jax.nn.sigmoid(c)
    o_ref[...] = jnp.dot(cond.astype(BF16), w_ref[...].astype(BF16),
                         preferred_element_type=F32) + b_ref[...]


def _ada(c_pad, w_ada, b_ada, *, tn=1024):
    rows, d = c_pad.shape
    n = w_ada.shape[-1]
    return pl.pallas_call(
        _ada_kernel,
        grid=(n // tn,),
        in_specs=[pl.BlockSpec((rows, d), lambda j: (0, 0)),
                  pl.BlockSpec((None, d, tn), lambda j: (0, 0, j)),
                  pl.BlockSpec((1, tn), lambda j: (0, j))],
        out_specs=pl.BlockSpec((rows, tn), lambda j: (0, j)),
        out_shape=jax.ShapeDtypeStruct((rows, n), F32),
        compiler_params=_params(("arbitrary",), 40),
        name="ada",
    )(c_pad, w_ada, b_ada)


def _inproj_kernel(x_ref, mod_ref, w_ref, o_ref, h_scr):
    @pl.when(pl.program_id(1) == 0)
    def _():
        shift = mod_ref[0:1, :]
        scale = mod_ref[1:2, :]
        h_scr[...] = (x_ref[...] * (1.0 + scale) + shift).astype(BF16)

    o_ref[...] = jnp.dot(h_scr[...], w_ref[...], preferred_element_type=F32).astype(BF16)


def _inproj(x2, mod, w_in, *, seq, tm=512, tn=1024):
    t, d = x2.shape
    n = w_in.shape[1]
    tm = min(tm, seq)
    per_b = seq // tm
    return pl.pallas_call(
        _inproj_kernel,
        grid=(t // tm, n // tn),
        in_specs=[pl.BlockSpec((tm, d), lambda i, j: (i, 0)),
                  pl.BlockSpec((None, 6, d), lambda i, j: (i // per_b, 0, 0)),
                  pl.BlockSpec((d, tn), lambda i, j: (0, j))],
        out_specs=pl.BlockSpec((tm, tn), lambda i, j: (i, j)),
        out_shape=jax.ShapeDtypeStruct((t, n), BF16),
        scratch_shapes=[pltpu.VMEM((tm, d), BF16)],
        compiler_params=_params(("arbitrary", "arbitrary"), 48),
        name="inproj",
    )(x2, mod, w_in)


def _sb_kernel(q_ref, k_ref, v_ref, tri_ref, o_ref, *, tq, tk, scale):
    qi = pl.program_id(1)
    nd = tq // tk
    q = q_ref[...]
    tri = tri_ref[...]
    row = lax.broadcasted_iota(jnp.int32, (tq, tk), 0)
    col = lax.broadcasted_iota(jnp.int32, (tq, tk), 1)

    def tile(j, acc, out, mask):
        ks = pl.multiple_of(j * tk, tk)
        kb = k_ref[pl.ds(ks, tk), :]
        vb = v_ref[pl.ds(ks, tk), :]
        z = lax.dot_general(q, kb, _NT, preferred_element_type=F32) * scale
        sp = jnp.maximum(z, 0.0) + jnp.log(1.0 + jnp.exp(-jnp.abs(z)))
        l1mb = -sp
        if mask is not None:
            l1mb = jnp.where(mask, l1mb, 0.0)
        cs = jnp.dot(l1mb.astype(BF16), tri, preferred_element_type=F32)
        w = jnp.exp(z - sp + cs[:, :tk] + acc)
        if mask is not None:
            w = jnp.where(mask, w, 0.0)
        out = out + jnp.dot(w.astype(BF16), vb, preferred_element_type=F32)
        acc = acc + cs[:, tk:]
        return acc, out

    acc = jnp.zeros((tq, tk), F32)
    out = jnp.zeros((tq, LANES), F32)
    for dd in reversed(range(nd)):
        acc, out = tile(qi * nd + dd, acc, out, (col + dd * tk) < row)

    def body(i, carry):
        a, o = carry
        j = qi * nd - 1 - i * nd
        for u in range(nd):
            a, o = tile(j - u, a, o, None)
        return a, o

    acc, out = lax.fori_loop(0, qi, body, (acc, out))
    o_ref[...] = out.astype(o_ref.dtype)


def _sbattn(proj, tri, *, batch, seq, tq=256, tk=128):
    t = proj.shape[0]
    nq = seq // tq
    kernel = functools.partial(_sb_kernel, tq=tq, tk=tk, scale=1.0 / math.sqrt(LANES))
    return pl.pallas_call(
        kernel,
        grid=(batch * SB_HEADS, nq),
        in_specs=[pl.BlockSpec((tq, LANES), lambda bh, qi: ((bh // SB_HEADS) * nq + qi, bh % SB_HEADS)),
                  pl.BlockSpec((seq, LANES), lambda bh, qi: (bh // SB_HEADS, SB_HEADS + bh % SB_HEADS)),
                  pl.BlockSpec((seq, LANES), lambda bh, qi: (bh // SB_HEADS, 2 * SB_HEADS + bh % SB_HEADS)),
                  pl.BlockSpec((tk, 2 * tk), lambda bh, qi: (0, 0))],
        out_specs=pl.BlockSpec((tq, LANES), lambda bh, qi: ((bh // SB_HEADS) * nq + qi, bh % SB_HEADS)),
        out_shape=jax.ShapeDtypeStruct((t, SB_HEADS * LANES), BF16),
        compiler_params=_params(("arbitrary", "arbitrary"), 32),
        name="sbattn",
    )(proj, proj, proj, tri)


def _hgrn_kernel(q_ref, f_ref, i_ref, g_ref, lb_ref, ng_ref, ltri_ref, ones_ref, o_ref, st_ref,
                 *, rb, c):
    @pl.when(pl.program_id(1) == 0)
    def _():
        st_ref[...] = jnp.zeros_like(st_ref)

    lb = lb_ref[...]
    ng = ng_ref[...]
    ltri = ltri_ref[...]
    ones = ones_ref[...]
    ngroups = c // 8
    lane8 = lax.broadcasted_iota(jnp.int32, (8, LANES), 1)
    sub8 = lax.broadcasted_iota(jnp.int32, (8, LANES), 0)

    pre = []
    for ci in range(rb // c):
        r0 = ci * c
        qf = q_ref[r0:r0 + c, :].astype(F32)
        ff = f_ref[r0:r0 + c, :].astype(F32)
        vb = i_ref[r0:r0 + c, :]
        fg = lb + (1.0 - lb) * jax.nn.sigmoid(ff)
        lf = jnp.log(fg)
        kk = 1.0 - fg
        hi = lf.astype(BF16)
        lo = (lf - hi.astype(F32)).astype(BF16)
        b = (jnp.dot(ltri, hi, preferred_element_type=F32)
             + jnp.dot(ltri, lo, preferred_element_type=F32))

        slabs = []
        for s in range(c):
            g0 = (s // 8) * 8
            p = qf[g0:, :] * kk[s:s + 1, :] * jnp.exp(b[g0:, :] - b[s:s + 1, :])
            if s % 8:
                head = jnp.where(sub8 >= (s % 8), p[:8, :], 0.0)
                p = head if c - g0 == 8 else jnp.concatenate([head, p[8:, :]], axis=0)
            slabs.append(p.astype(BF16))
        red = jnp.dot(jnp.concatenate(slabs, axis=0), ones, preferred_element_type=F32)
        groups = [jnp.zeros((8, LANES), F32) for _ in range(ngroups)]
        off = 0
        for s in range(c):
            for g in range(s // 8, ngroups):
                groups[g] = jnp.where(lane8 == s, red[off:off + 8, :], groups[g])
                off += 8
        scores = jnp.concatenate(groups, axis=0)[:, :c]

        o_intra = jnp.dot(scores.astype(BF16), vb, preferred_element_type=F32)
        b_last = b[c - 1:c, :]
        qt = (qf * jnp.exp(b)).astype(BF16)
        kt = (kk * jnp.exp(b_last - b)).astype(BF16)
        upd = lax.dot_general(vb, kt, _TN, preferred_element_type=F32)
        pre.append((o_intra, qt, upd, jnp.exp(b_last)))

    st = st_ref[...]
    for ci, (o_intra, qt, upd, dec) in enumerate(pre):
        r0 = ci * c
        o = o_intra + lax.dot_general(qt, st.astype(BF16), _NT, preferred_element_type=F32)
        st = st * dec + upd
        o = o * lax.rsqrt(jnp.mean(o * o, axis=-1, keepdims=True) + RMS_EPS)
        gg = g_ref[r0:r0 + c, :].astype(F32)
        o_ref[r0:r0 + c, :] = (o * ng * (gg * jax.nn.sigmoid(gg))).astype(o_ref.dtype)
    st_ref[...] = st


def _hgrn(proj, lb, ng, ltri, ones, *, batch, seq, rb=256):
    t = proj.shape[0]
    nr = seq // rb
    base = 3 * SB_HEADS

    def col(k):
        return pl.BlockSpec((rb, LANES),
                            lambda bh, r: ((bh // HG_HEADS) * nr + r, base + k * HG_HEADS + bh % HG_HEADS))

    vec = pl.BlockSpec((None, 1, LANES), lambda bh, r: (bh % HG_HEADS, 0, 0))
    kernel = functools.partial(_hgrn_kernel, rb=rb, c=HG_CHUNK)
    return pl.pallas_call(
        kernel,
        grid=(batch * HG_HEADS, nr),
        in_specs=[col(0), col(1), col(2), col(3), vec, vec,
                  pl.BlockSpec((HG_CHUNK, HG_CHUNK), lambda bh, r: (0, 0)),
                  pl.BlockSpec((LANES, LANES), lambda bh, r: (0, 0))],
        out_specs=pl.BlockSpec((rb, LANES), lambda bh, r: ((bh // HG_HEADS) * nr + r, bh % HG_HEADS)),
        out_shape=jax.ShapeDtypeStruct((t, HG_HEADS * LANES), BF16),
        scratch_shapes=[pltpu.VMEM((LANES, LANES), F32)],
        compiler_params=_params(("arbitrary", "arbitrary"), 32),
        name="hgrn",
    )(proj, proj, proj, proj, lb, ng, ltri, ones)


def _layer_norm(z, g, b):
    mu = jnp.mean(z, axis=-1, keepdims=True)
    zc = z - mu
    var = jnp.mean(zc * zc, axis=-1, keepdims=True)
    return zc * lax.rsqrt(var + LN_EPS) * g + b


def _mix_kernel(a_ref, b_ref, ga0_ref, ga1_ref, gb0_ref, gb1_ref, x_ref, mod_ref, wa_ref, wb_ref,
                wo_ref, g_ref, be_ref, x1_ref, h2_ref, *, alpha):
    a = a_ref[...]
    b = b_ref[...]
    half = wa_ref.shape[1] // 2
    merged = []
    for n, (ga_ref, gb_ref) in enumerate(((ga0_ref, gb0_ref), (ga1_ref, gb1_ref))):
        pa = jnp.dot(a, wa_ref[:, n * half:(n + 1) * half], preferred_element_type=F32)
        pb = jnp.dot(b, wb_ref[:, n * half:(n + 1) * half], preferred_element_type=F32)
        m = (jax.nn.sigmoid(ga_ref[...].astype(F32)) * pa
             + jax.nn.sigmoid(gb_ref[...].astype(F32)) * pb)
        merged.append(m.astype(BF16))
    y = jnp.dot(jnp.concatenate(merged, axis=1), wo_ref[...], preferred_element_type=F32)
    gate1 = mod_ref[2:3, :]
    shift2 = mod_ref[3:4, :]
    scale2 = mod_ref[4:5, :]
    x1 = _layer_norm(alpha * x_ref[...] + (1.0 + gate1) * y, g_ref[...], be_ref[...])
    x1_ref[...] = x1
    h2_ref[...] = (x1 * (1.0 + scale2) + shift2).astype(BF16)


def _mix(a_out, b_out, proj, x2, mod, wa, wb, wo, ln_g, ln_b, *, seq, alpha, tm=256):
    t, d = x2.shape
    w = a_out.shape[1]
    tm = min(tm, seq)
    per_b = seq // tm
    gbase = (3 * SB_HEADS + 4 * HG_HEADS) * LANES // w
    row = lambda i: (i, 0)
    const = lambda i: (0, 0)

    def gate(k):
        return pl.BlockSpec((tm, w), lambda i: (i, gbase + k))

    kernel = functools.partial(_mix_kernel, alpha=alpha)
    return pl.pallas_call(
        kernel,
        grid=(t // tm,),
        in_specs=[pl.BlockSpec((tm, w), row), pl.BlockSpec((tm, w), row),
                  gate(0), gate(1), gate(2), gate(3),
                  pl.BlockSpec((tm, d), row),
                  pl.BlockSpec((None, 6, d), lambda i: (i // per_b, 0, 0)),
                  pl.BlockSpec((w, d), const), pl.BlockSpec((w, d), const), pl.BlockSpec((d, d), const),
                  pl.BlockSpec((1, d), const), pl.BlockSpec((1, d), const)],
        out_specs=[pl.BlockSpec((tm, d), row), pl.BlockSpec((tm, d), row)],
        out_shape=[jax.ShapeDtypeStruct((t, d), F32), jax.ShapeDtypeStruct((t, d), BF16)],
        compiler_params=_params(("arbitrary",), 56),
        name="mix",
    )(a_out, b_out, proj, proj, proj, proj, x2, mod, wa, wb, wo, ln_g, ln_b)


def _top_rows(vals, payloads, k, n_rows):
    rid = lax.broadcasted_iota(jnp.int32, vals.shape, 0).astype(F32)
    best, outs = [], [[] for _ in payloads]
    for _ in range(k):
        m = jnp.max(vals, axis=0, keepdims=True)
        first = jnp.min(jnp.where(vals == m, rid, float(n_rows)), axis=0, keepdims=True)
        sel = rid == first
        best.append(m)
        for o, p in zip(outs, payloads):
            o.append(jnp.sum(jnp.where(sel, p, 0.0), axis=0, keepdims=True))
        vals = jnp.where(sel, -jnp.inf, vals)
    return best, outs


def _route_kernel(h_ref, wq_ref, keys_ref, ei_ref, ej_ref, gt_ref):
    tb = h_ref.shape[0]
    k = PEER_TOPK
    q = jnp.dot(h_ref[...], wq_ref[...], preferred_element_type=F32).astype(BF16)
    rid = lax.broadcasted_iota(jnp.int32, (N_KEYS, tb), 0).astype(F32)
    ei, ej, gt = [], [], []
    for h in range(PEER_HEADS):
        tops, idxs = [], []
        for p in range(2):
            c0 = (h * 2 + p) * LANES
            s = lax.dot_general(keys_ref[h * 2 + p], q[:, c0:c0 + LANES], _NT,
                                preferred_element_type=F32)
            best, (idx,) = _top_rows(s, [rid], k, N_KEYS)
            tops.append(best)
            idxs.append(idx)
        top1 = jnp.concatenate(tops[1], axis=0)
        idx1 = jnp.concatenate(idxs[1], axis=0)
        cand = jnp.concatenate([tops[0][a] + top1 for a in range(k)], axis=0)
        ci = jnp.concatenate([jnp.broadcast_to(idxs[0][a], (k, tb)) for a in range(k)], axis=0)
        cj = jnp.concatenate([idx1] * k, axis=0)
        best, (bi, bj) = _top_rows(cand, [ci, cj], k, k * k)
        e = [jnp.exp(v - best[0]) for v in best]
        denom = functools.reduce(lambda a, b: a + b, e)
        inv = 1.0 / denom
        ei.extend(bi)
        ej.extend(bj)
        gt.extend([v * inv for v in e])
    ei_ref[...] = jnp.concatenate(ei, axis=0).T
    ej_ref[...] = jnp.concatenate(ej, axis=0).T
    gt_ref[...] = jnp.concatenate(gt, axis=0).T


def _route(h2, wq, keys, *, tb=256):
    t, d = h2.shape
    tb = min(tb, t)
    nk = PEER_HEADS * PEER_TOPK
    out = jax.ShapeDtypeStruct((t, nk), F32)
    ospec = pl.BlockSpec((tb, nk), lambda i: (i, 0))
    return pl.pallas_call(
        _route_kernel,
        grid=(t // tb,),
        in_specs=[pl.BlockSpec((tb, d), lambda i: (i, 0)),
                  pl.BlockSpec(wq.shape, lambda i: (0, 0)),
                  pl.BlockSpec(keys.shape, lambda i: (0, 0, 0))],
        out_specs=[ospec, ospec, ospec],
        out_shape=[out, out, out],
        compiler_params=_params(("arbitrary",), 48),
        name="route",
    )(h2, wq, keys)


def _scatter_kernel(ei_ref, ej_ref, gt_ref, o_ref, scr, *, tb):
    sub = lax.broadcasted_iota(jnp.int32, (N_KEYS, LANES), 0).astype(F32)

    def body(t, carry):
        irow = ei_ref[pl.ds(t, 1), :]
        jrow = ej_ref[pl.ds(t, 1), :]
        grow = gt_ref[pl.ds(t, 1), :]
        r = jnp.where(sub == irow, grow, 0.0).astype(BF16)
        cm = jnp.where(sub == jrow, 1.0, 0.0).astype(BF16)
        scr[pl.ds(pl.multiple_of(t * N_KEYS, N_KEYS), N_KEYS), :] = lax.dot_general(
            r, cm, _NT, preferred_element_type=F32)
        return carry

    lax.fori_loop(0, tb, body, 0, unroll=4)
    for i in range(N_KEYS):
        o_ref[:, i * N_KEYS:(i + 1) * N_KEYS] = scr[pl.ds(i, tb, stride=N_KEYS), :].astype(o_ref.dtype)


def _scatter(ei, ej, gt, *, tb=128):
    t, nk = ei.shape
    tb = min(tb, t)
    spec = pl.BlockSpec((tb, nk), lambda i: (i, 0))
    kernel = functools.partial(_scatter_kernel, tb=tb)
    return pl.pallas_call(
        kernel,
        grid=(t // tb,),
        in_specs=[spec, spec, spec],
        out_specs=pl.BlockSpec((tb, N_KEYS * N_KEYS), lambda i: (i, 0)),
        out_shape=jax.ShapeDtypeStruct((t, N_KEYS * N_KEYS), BF16),
        scratch_shapes=[pltpu.VMEM((tb * N_KEYS, N_KEYS), F32)],
        compiler_params=_params(("arbitrary",), 40),
        name="scatter",
    )(ei, ej, gt)


def _peer_kernel(h_ref, u_ref, v_ref, gm_ref, x1_ref, mod_ref, g_ref, be_ref, o_ref, acc_ref, *, alpha):
    j = pl.program_id(1)

    @pl.when(j == 0)
    def _():
        acc_ref[...] = jnp.zeros_like(acc_ref)

    act = lax.dot_general(h_ref[...], u_ref[...], _NT, preferred_element_type=F32)
    gelu = 0.5 * act * (1.0 + lax.erf(act * (1.0 / math.sqrt(2.0))))
    w = (gm_ref[...].astype(F32) * gelu).astype(BF16)
    acc_ref[...] += jnp.dot(w, v_ref[...], preferred_element_type=F32)

    @pl.when(j == pl.num_programs(1) - 1)
    def _():
        gate2 = mod_ref[5:6, :]
        o_ref[...] = _layer_norm(alpha * x1_ref[...] + (1.0 + gate2) * acc_ref[...],
                                 g_ref[...], be_ref[...])


def _peer(h2, u, v, gm, x1, mod, ln_g, ln_b, *, seq, alpha, tb=512, eb=1024):
    t, d = h2.shape
    ne = u.shape[0]
    tb = min(tb, seq)
    per_b = seq // tb
    row = lambda i, j: (i, 0)
    const = lambda i, j: (0, 0)
    kernel = functools.partial(_peer_kernel, alpha=alpha)
    return pl.pallas_call(
        kernel,
        grid=(t // tb, ne // eb),
        in_specs=[pl.BlockSpec((tb, d), row),
                  pl.BlockSpec((eb, d), lambda i, j: (j, 0)),
                  pl.BlockSpec((eb, d), lambda i, j: (j, 0)),
                  pl.BlockSpec((tb, eb), lambda i, j: (i, j)),
                  pl.BlockSpec((tb, d), row),
                  pl.BlockSpec((None, 6, d), lambda i, j: (i // per_b, 0, 0)),
                  pl.BlockSpec((1, d), const), pl.BlockSpec((1, d), const)],
        out_specs=pl.BlockSpec((tb, d), row),
        out_shape=jax.ShapeDtypeStruct((t, d), F32),
        scratch_shapes=[pltpu.VMEM((tb, d), F32)],
        compiler_params=_params(("arbitrary", "arbitrary"), 56),
        name="peer",
    )(h2, u, v, gm, x1, mod, ln_g, ln_b)


def kernel(x, c, w_ada, b_ada, w_in, hg_lb_logits, hg_norm_g, w_branch_a, w_branch_b, w_out,
           ln1_g, ln1_b, peer_wq, peer_subkeys, peer_u, peer_v, ln2_g, ln2_b):
    batch, seq, d = x.shape
    depth = w_ada.shape[0]
    alpha = (2.0 * depth) ** 0.25
    lower_bounds = jnp.cumsum(jax.nn.softmax(hg_lb_logits.astype(F32), axis=0), axis=0)

    r = lax.broadcasted_iota(jnp.int32, (LANES, LANES), 0)
    cc = lax.broadcasted_iota(jnp.int32, (LANES, LANES), 1)
    tri = jnp.concatenate([(r > cc), jnp.ones((LANES, LANES), bool)], axis=1).astype(BF16)
    ltri = (r[:HG_CHUNK, :HG_CHUNK] >= cc[:HG_CHUNK, :HG_CHUNK]).astype(BF16)
    ones = jnp.ones((LANES, LANES), BF16)

    c_pad = jnp.zeros((8, d), F32).at[:batch].set(c)
    xt = x.reshape(batch * seq, d)
    for layer in range(depth):
        mod = _ada(c_pad, w_ada[layer:layer + 1], b_ada[layer:layer + 1])[:batch].reshape(batch, 6, d)
        proj = _inproj(xt, mod, w_in[layer].astype(BF16), seq=seq)
        a_out = _sbattn(proj, tri, batch=batch, seq=seq)
        b_out = _hgrn(proj, lower_bounds[layer].reshape(HG_HEADS, 1, LANES),
                      hg_norm_g[layer].reshape(HG_HEADS, 1, LANES), ltri, ones, batch=batch, seq=seq)
        x1, h2 = _mix(a_out, b_out, proj, xt, mod, w_branch_a[layer].astype(BF16),
                      w_branch_b[layer].astype(BF16), w_out[layer].astype(BF16),
                      ln1_g[layer:layer + 1], ln1_b[layer:layer + 1], seq=seq, alpha=alpha)
        keys = peer_subkeys[layer].reshape(PEER_HEADS * 2, N_KEYS, LANES).astype(BF16)
        ei, ej, gt = _route(h2, peer_wq[layer].astype(BF16), keys)
        gm = _scatter(ei, ej, gt)
        xt = _peer(h2, peer_u[layer].astype(BF16), peer_v[layer].astype(BF16), gm, x1, mod,
                   ln2_g[layer:layer + 1], ln2_b[layer:layer + 1], seq=seq, alpha=alpha)
    return xt.reshape(batch, seq, d)
```

```python
import functools
import math

import jax
import jax.numpy as jnp
from jax import lax
from jax.experimental import pallas as pl
from jax.experimental.pallas import tpu as pltpu

F32 = jnp.float32
BF16 = jnp.bfloat16

LANES = 128
SB_HEADS = 8
SB_QBLOCK = 512
SB_KBLOCK = 256
HG_HEADS = 8
HG_CHUNK = 64
HG_SUB = 16
LOG2E = 1.4426950408889634
PEER_HEADS = 8
PEER_TOPK = 16
N_KEYS = 128
RMS_EPS = 1e-6
LN_EPS = 1e-5
MIB = 1024 * 1024

_NT = (((1,), (1,)), ((), ()))
_TN = (((0,), (0,)), ((), ()))


def _params(semantics, vmem_mib):
    return pltpu.CompilerParams(dimension_semantics=semantics, vmem_limit_bytes=vmem_mib * MIB)


def _ada_kernel(c_ref, w_ref, b_ref, o_ref):
    c = c_ref[...]
    cond = c * jax.nn.sigmoid(c)
    o_ref[...] = jnp.dot(cond.astype(BF16), w_ref[...].astype(BF16),
                         preferred_element_type=F32) + b_ref[...]


def _ada(c_pad, w_ada, b_ada, *, tn=1024):
    rows, d = c_pad.shape
    n = w_ada.shape[-1]
    return pl.pallas_call(
        _ada_kernel,
        grid=(n // tn,),
        in_specs=[pl.BlockSpec((rows, d), lambda j: (0, 0)),
                  pl.BlockSpec((None, d, tn), lambda j: (0, 0, j)),
                  pl.BlockSpec((1, tn), lambda j: (0, j))],
        out_specs=pl.BlockSpec((rows, tn), lambda j: (0, j)),
        out_shape=jax.ShapeDtypeStruct((rows, n), F32),
        compiler_params=_params(("arbitrary",), 40),
        name="ada",
    )(c_pad, w_ada, b_ada)


def _inproj_kernel(x_ref, mod_ref, w_ref, o_ref, h_scr):
    @pl.when(pl.program_id(1) == 0)
    def _():
        shift = mod_ref[0:1, :]
        scale = mod_ref[1:2, :]
        h_scr[...] = (x_ref[...] * (1.0 + scale) + shift).astype(BF16)

    o_ref[...] = jnp.dot(h_scr[...], w_ref[...], preferred_element_type=F32).astype(BF16)


def _inproj(x2, mod, w_in, *, seq, tm=512, tn=1024):
    t, d = x2.shape
    n = w_in.shape[1]
    tm = min(tm, seq)
    per_b = seq // tm
    return pl.pallas_call(
        _inproj_kernel,
        grid=(t // tm, n // tn),
        in_specs=[pl.BlockSpec((tm, d), lambda i, j: (i, 0)),
                  pl.BlockSpec((None, 6, d), lambda i, j: (i // per_b, 0, 0)),
                  pl.BlockSpec((d, tn), lambda i, j: (0, j))],
        out_specs=pl.BlockSpec((tm, tn), lambda i, j: (i, j)),
        out_shape=jax.ShapeDtypeStruct((t, n), BF16),
        scratch_shapes=[pltpu.VMEM((tm, d), BF16)],
        compiler_params=_params(("arbitrary", "arbitrary"), 48),
        name="inproj",
    )(x2, mod, w_in)


def _sb_kernel(q_ref, k_ref, v_ref, tri_ref, o_ref, *, tq, tk, scale):
    qi = pl.program_id(1)
    nd = tq // tk
    q = q_ref[...]
    tri = tri_ref[...]
    row = lax.broadcasted_iota(jnp.int32, (tq, tk), 0)
    col = lax.broadcasted_iota(jnp.int32, (tq, tk), 1)

    def tile(j, acc, out, mask):
        ks = pl.multiple_of(j * tk, tk)
        zr = lax.dot_general(q, k_ref[pl.ds(ks, tk), :], _NT, preferred_element_type=F32)
        z = zr * scale
        lp = jnp.log(1.0 + jnp.exp(jnp.abs(zr) * (-scale)))
        logb = jnp.minimum(z, 0.0) - lp
        l1mb = logb - z
        if mask is not None:
            l1mb = jnp.where(mask, l1mb, 0.0)
            logb = jnp.where(mask, logb, -1e30)
        cs = jnp.dot(l1mb.astype(BF16), tri, preferred_element_type=F32)
        w = jnp.exp(logb + cs[:, :tk] + jnp.concatenate([acc] * (tk // LANES), axis=1))
        out = out + jnp.dot(w.astype(BF16), v_ref[pl.ds(ks, tk), :], preferred_element_type=F32)
        return acc + cs[:, tk:], out

    acc = jnp.zeros((tq, LANES), F32)
    out = jnp.zeros((tq, LANES), F32)
    for dd in reversed(range(nd)):
        acc, out = tile(qi * nd + dd, acc, out, (col + dd * tk) < row)

    def body(i, carry):
        a, o = carry
        j = (qi - i) * nd - 1
        for u in range(nd):
            a, o = tile(j - u, a, o, None)
        return a, o

    acc, out = lax.fori_loop(0, qi, body, (acc, out))
    o_ref[...] = out.astype(o_ref.dtype)


def _sbattn(proj, tri, *, batch, seq, tq=SB_QBLOCK, tk=SB_KBLOCK):
    t = proj.shape[0]
    tq = min(tq, seq)
    nq = seq // tq
    kernel = functools.partial(_sb_kernel, tq=tq, tk=tk, scale=1.0 / math.sqrt(LANES))
    return pl.pallas_call(
        kernel,
        grid=(batch * SB_HEADS, nq),
        in_specs=[pl.BlockSpec((tq, LANES), lambda bh, qi: ((bh // SB_HEADS) * nq + qi, bh % SB_HEADS)),
                  pl.BlockSpec((seq, LANES), lambda bh, qi: (bh // SB_HEADS, SB_HEADS + bh % SB_HEADS)),
                  pl.BlockSpec((seq, LANES), lambda bh, qi: (bh // SB_HEADS, 2 * SB_HEADS + bh % SB_HEADS)),
                  pl.BlockSpec((tk, tk + LANES), lambda bh, qi: (0, 0))],
        out_specs=pl.BlockSpec((tq, LANES), lambda bh, qi: ((bh // SB_HEADS) * nq + qi, bh % SB_HEADS)),
        out_shape=jax.ShapeDtypeStruct((t, SB_HEADS * LANES), BF16),
        compiler_params=_params(("arbitrary", "arbitrary"), 40),
        name="sbattn",
    )(proj, proj, proj, tri)


def _hgrn_kernel(q_ref, f_ref, i_ref, g_ref, lb_ref, ng_ref, ltri_ref, ones_ref, o_ref, st_ref,
                 *, rb, c, sb):
    @pl.when(pl.program_id(1) == 0)
    def _():
        st_ref[...] = jnp.zeros_like(st_ref)

    lb = lb_ref[...]
    ng = ng_ref[...]
    ltri = ltri_ref[...]
    ones = ones_ref[...]
    lane8 = lax.broadcasted_iota(jnp.int32, (8, c), 1)
    sub8 = lax.broadcasted_iota(jnp.int32, (8, LANES), 0)

    pre = []
    for ci in range(rb // c):
        r0 = ci * c
        qf = q_ref[r0:r0 + c, :].astype(F32)
        ff = f_ref[r0:r0 + c, :].astype(F32)
        vb = i_ref[r0:r0 + c, :]
        fg = lb + (1.0 - lb) * jax.nn.sigmoid(ff)
        lf = jnp.log(fg)
        kk = 1.0 - fg
        hi = lf.astype(BF16)
        lo = (lf - hi.astype(F32)).astype(BF16)
        b = (jnp.dot(ltri, hi, preferred_element_type=F32)
             + jnp.dot(ltri, lo, preferred_element_type=F32))

        b2 = b * LOG2E

        groups = [jnp.zeros((8, c), F32), jnp.zeros((8, c), F32)]
        for i in range(1, c // sb):
            r = b2[i * sb - 1:i * sb, :]
            qs = (qf[i * sb:(i + 1) * sb, :] * jnp.exp2(b2[i * sb:(i + 1) * sb, :] - r)).astype(BF16)
            ks = (kk[:i * sb, :] * jnp.exp2(r - b2[:i * sb, :])).astype(BF16)
            ks = jnp.concatenate([ks, jnp.zeros((c - i * sb, LANES), BF16)], axis=0)
            off_diag = lax.dot_general(qs, ks, _NT, preferred_element_type=F32)
            groups.extend([off_diag[:8, :], off_diag[8:, :]])

        slabs = []
        for s in range(c):
            g0, end = (s // 8) * 8, (s // sb + 1) * sb
            p = qf[g0:end, :] * kk[s:s + 1, :] * jnp.exp2(b2[g0:end, :] - b2[s:s + 1, :])
            if s % 8:
                head = jnp.where(sub8 >= (s % 8), p[:8, :], 0.0)
                p = head if end - g0 == 8 else jnp.concatenate([head, p[8:, :]], axis=0)
            slabs.append(p)
        red = jnp.dot(jnp.concatenate(slabs, axis=0).astype(BF16), ones, preferred_element_type=F32)
        off = 0
        for s in range(c):
            for g in range(s // 8, (s // sb + 1) * (sb // 8)):
                groups[g] = jnp.where(lane8 == s, red[off:off + 8, :c], groups[g])
                off += 8
        scores = jnp.concatenate(groups, axis=0)

        o_intra = jnp.dot(scores.astype(BF16), vb, preferred_element_type=F32)
        b_last = b2[c - 1:c, :]
        qt = (qf * jnp.exp2(b2)).astype(BF16)
        kt = (kk * jnp.exp2(b_last - b2)).astype(BF16)
        upd = lax.dot_general(vb, kt, _TN, preferred_element_type=F32)
        pre.append((o_intra, qt, upd, jnp.exp2(b_last)))

    st = st_ref[...]
    for ci, (o_intra, qt, upd, dec) in enumerate(pre):
        r0 = ci * c
        o = o_intra + lax.dot_general(qt, st.astype(BF16), _NT, preferred_element_type=F32)
        st = st * dec + upd
        o = o * lax.rsqrt(jnp.mean(o * o, axis=-1, keepdims=True) + RMS_EPS)
        gg = g_ref[r0:r0 + c, :].astype(F32)
        o_ref[r0:r0 + c, :] = (o * ng * (gg * jax.nn.sigmoid(gg))).astype(o_ref.dtype)
    st_ref[...] = st


def _hgrn(proj, lb, ng, ltri, ones, *, batch, seq, rb=1024):
    t = proj.shape[0]
    rb = min(rb, seq)
    nr = seq // rb
    base = 3 * SB_HEADS

    def col(k):
        return pl.BlockSpec((rb, LANES),
                            lambda bh, r: ((bh // HG_HEADS) * nr + r, base + k * HG_HEADS + bh % HG_HEADS))

    vec = pl.BlockSpec((None, 1, LANES), lambda bh, r: (bh % HG_HEADS, 0, 0))
    kernel = functools.partial(_hgrn_kernel, rb=rb, c=HG_CHUNK, sb=HG_SUB)
    return pl.pallas_call(
        kernel,
        grid=(batch * HG_HEADS, nr),
        in_specs=[col(0), col(1), col(2), col(3), vec, vec,
                  pl.BlockSpec((HG_CHUNK, HG_CHUNK), lambda bh, r: (0, 0)),
                  pl.BlockSpec((LANES, LANES), lambda bh, r: (0, 0))],
        out_specs=pl.BlockSpec((rb, LANES), lambda bh, r: ((bh // HG_HEADS) * nr + r, bh % HG_HEADS)),
        out_shape=jax.ShapeDtypeStruct((t, HG_HEADS * LANES), BF16),
        scratch_shapes=[pltpu.VMEM((LANES, LANES), F32)],
        compiler_params=_params(("arbitrary", "arbitrary"), 32),
        name="hgrn",
    )(proj, proj, proj, proj, lb, ng, ltri, ones)


def _layer_norm(z, g, b):
    mu = jnp.mean(z, axis=-1, keepdims=True)
    zc = z - mu
    var = jnp.mean(zc * zc, axis=-1, keepdims=True)
    return zc * lax.rsqrt(var + LN_EPS) * g + b


def _mix_kernel(a_ref, b_ref, ga0_ref, ga1_ref, gb0_ref, gb1_ref, x_ref, mod_ref, wa_ref, wb_ref,
                wo_ref, g_ref, be_ref, x1_ref, h2_ref, *, alpha):
    a = a_ref[...]
    b = b_ref[...]
    half = wa_ref.shape[1] // 2
    merged = []
    for n, (ga_ref, gb_ref) in enumerate(((ga0_ref, gb0_ref), (ga1_ref, gb1_ref))):
        pa = jnp.dot(a, wa_ref[:, n * half:(n + 1) * half], preferred_element_type=F32)
        pb = jnp.dot(b, wb_ref[:, n * half:(n + 1) * half], preferred_element_type=F32)
        m = (jax.nn.sigmoid(ga_ref[...].astype(F32)) * pa
             + jax.nn.sigmoid(gb_ref[...].astype(F32)) * pb)
        merged.append(m.astype(BF16))
    y = jnp.dot(jnp.concatenate(merged, axis=1), wo_ref[...], preferred_element_type=F32)
    gate1 = mod_ref[2:3, :]
    shift2 = mod_ref[3:4, :]
    scale2 = mod_ref[4:5, :]
    x1 = _layer_norm(alpha * x_ref[...] + (1.0 + gate1) * y, g_ref[...], be_ref[...])
    x1_ref[...] = x1
    h2_ref[...] = (x1 * (1.0 + scale2) + shift2).astype(BF16)


def _mix(a_out, b_out, proj, x2, mod, wa, wb, wo, ln_g, ln_b, *, seq, alpha, tm=256):
    t, d = x2.shape
    w = a_out.shape[1]
    tm = min(tm, seq)
    per_b = seq // tm
    gbase = (3 * SB_HEADS + 4 * HG_HEADS) * LANES // w
    row = lambda i: (i, 0)
    const = lambda i: (0, 0)

    def gate(k):
        return pl.BlockSpec((tm, w), lambda i: (i, gbase + k))

    kernel = functools.partial(_mix_kernel, alpha=alpha)
    return pl.pallas_call(
        kernel,
        grid=(t // tm,),
        in_specs=[pl.BlockSpec((tm, w), row), pl.BlockSpec((tm, w), row),
                  gate(0), gate(1), gate(2), gate(3),
                  pl.BlockSpec((tm, d), row),
                  pl.BlockSpec((None, 6, d), lambda i: (i // per_b, 0, 0)),
                  pl.BlockSpec((w, d), const), pl.BlockSpec((w, d), const), pl.BlockSpec((d, d), const),
                  pl.BlockSpec((1, d), const), pl.BlockSpec((1, d), const)],
        out_specs=[pl.BlockSpec((tm, d), row), pl.BlockSpec((tm, d), row)],
        out_shape=[jax.ShapeDtypeStruct((t, d), F32), jax.ShapeDtypeStruct((t, d), BF16)],
        compiler_params=_params(("arbitrary",), 56),
        name="mix",
    )(a_out, b_out, proj, proj, proj, proj, x2, mod, wa, wb, wo, ln_g, ln_b)


def _top_rows(vals, payloads, k, n_rows):
    rid = lax.broadcasted_iota(jnp.int32, vals.shape, 0).astype(F32)
    best, rows, outs = [], [], [[] for _ in payloads]
    for _ in range(k):
        m = jnp.max(vals, axis=0, keepdims=True)
        first = jnp.min(jnp.where(vals == m, rid, float(n_rows)), axis=0, keepdims=True)
        sel = rid == first
        best.append(m)
        rows.append(first)
        for o, p in zip(outs, payloads):
            o.append(jnp.sum(jnp.where(sel, p, 0.0), axis=0, keepdims=True))
        vals = jnp.where(sel, -jnp.inf, vals)
    return best, rows, outs


def _route_kernel(h_ref, wq_ref, keys_ref, ei_ref, ej_ref, gt_ref):
    tb = h_ref.shape[0]
    k = PEER_TOPK
    q = jnp.dot(h_ref[...], wq_ref[...], preferred_element_type=F32).astype(BF16)
    sub8 = lax.broadcasted_iota(jnp.int32, (8, tb), 0)
    ei, ej, gt = [], [], []
    for h in range(PEER_HEADS):
        tops, idxs = [], []
        for p in range(2):
            c0 = (h * 2 + p) * LANES
            s = lax.dot_general(keys_ref[h * 2 + p], q[:, c0:c0 + LANES], _NT,
                                preferred_element_type=F32)
            best, rows, _ = _top_rows(s, [], k, N_KEYS)
            tops.append(best)
            idxs.append(rows)
        top1 = jnp.concatenate(tops[1], axis=0)
        idx1 = jnp.concatenate(idxs[1], axis=0)
        cand = [tops[0][0] + top1]
        ci = [jnp.broadcast_to(idxs[0][0], (k, tb))]
        cj = [idx1]
        for a in range(1, 8):
            ca = tops[0][a] + top1[:8]
            if k // (a + 1) < 8:
                ca = jnp.where(sub8 < k // (a + 1), ca, -jnp.inf)
            cand.append(ca)
            ci.append(jnp.broadcast_to(idxs[0][a], (8, tb)))
            cj.append(idx1[:8])
        cand.append(jnp.concatenate(tops[0][8:], axis=0) + top1[:1])
        ci.append(jnp.concatenate(idxs[0][8:], axis=0))
        cj.append(jnp.broadcast_to(idx1[:1], (8, tb)))
        cand = jnp.concatenate(cand, axis=0)
        best, _, (bi, bj) = _top_rows(cand, [jnp.concatenate(ci, axis=0), jnp.concatenate(cj, axis=0)],
                                      k, cand.shape[0])
        e = [jnp.exp(v - best[0]) for v in best]
        denom = functools.reduce(lambda a, b: a + b, e)
        inv = 1.0 / denom
        ei.extend(bi)
        ej.extend(bj)
        gt.extend([v * inv for v in e])
    ei_ref[...] = jnp.concatenate(ei, axis=0).T
    ej_ref[...] = jnp.concatenate(ej, axis=0).T
    gt_ref[...] = jnp.concatenate(gt, axis=0).T


def _route(h2, wq, keys, *, tb=256):
    t, d = h2.shape
    tb = min(tb, t)
    nk = PEER_HEADS * PEER_TOPK
    out = jax.ShapeDtypeStruct((t, nk), F32)
    ospec = pl.BlockSpec((tb, nk), lambda i: (i, 0))
    return pl.pallas_call(
        _route_kernel,
        grid=(t // tb,),
        in_specs=[pl.BlockSpec((tb, d), lambda i: (i, 0)),
                  pl.BlockSpec(wq.shape, lambda i: (0, 0)),
                  pl.BlockSpec(keys.shape, lambda i: (0, 0, 0))],
        out_specs=[ospec, ospec, ospec],
        out_shape=[out, out, out],
        compiler_params=_params(("arbitrary",), 48),
        name="route",
    )(h2, wq, keys)


def _scatter_kernel(ei_ref, ej_ref, gt_ref, o_ref, *, tb, grp):
    sub = lax.broadcasted_iota(jnp.int32, (N_KEYS, LANES), 0).astype(F32).astype(BF16)
    one = jnp.ones((), BF16)
    zero = jnp.zeros((), BF16)

    def body(g, carry):
        t0 = pl.multiple_of(g * grp, grp)
        prods = []
        for u in range(grp):
            irow = ei_ref[pl.ds(t0 + u, 1), :].astype(BF16)
            jrow = ej_ref[pl.ds(t0 + u, 1), :].astype(BF16)
            grow = gt_ref[pl.ds(t0 + u, 1), :].astype(BF16)
            r = jnp.where(sub == irow, grow, zero)
            cm = jnp.where(sub == jrow, one, zero)
            prods.append(lax.dot_general(r, cm, _NT, preferred_element_type=F32))
        y = jnp.swapaxes(jnp.stack(prods, axis=0), 0, 1).astype(o_ref.dtype)
        for i in range(N_KEYS):
            o_ref[pl.ds(t0, grp), i * N_KEYS:(i + 1) * N_KEYS] = y[i]
        return carry

    lax.fori_loop(0, tb // grp, body, 0)


def _scatter(ei, ej, gt, *, tb=256, grp=16):
    t, nk = ei.shape
    tb = min(tb, t)
    spec = pl.BlockSpec((tb, nk), lambda i: (i, 0))
    kernel = functools.partial(_scatter_kernel, tb=tb, grp=grp)
    return pl.pallas_call(
        kernel,
        grid=(t // tb,),
        in_specs=[spec, spec, spec],
        out_specs=pl.BlockSpec((tb, N_KEYS * N_KEYS), lambda i: (i, 0)),
        out_shape=jax.ShapeDtypeStruct((t, N_KEYS * N_KEYS), BF16),
        compiler_params=_params(("arbitrary",), 40),
        name="scatter",
    )(ei, ej, gt)


def _peer_kernel(h_ref, u_ref, v_ref, gm_ref, x1_ref, mod_ref, g_ref, be_ref, o_ref, acc_ref, *, alpha):
    j = pl.program_id(1)

    @pl.when(j == 0)
    def _():
        acc_ref[...] = jnp.zeros_like(acc_ref)

    act = lax.dot_general(h_ref[...], u_ref[...], _NT, preferred_element_type=F32)
    gelu = 0.5 * act * (1.0 + lax.erf(act * (1.0 / math.sqrt(2.0))))
    w = (gm_ref[...].astype(F32) * gelu).astype(BF16)
    acc_ref[...] += jnp.dot(w, v_ref[...], preferred_element_type=F32)

    @pl.when(j == pl.num_programs(1) - 1)
    def _():
        gate2 = mod_ref[5:6, :]
        o_ref[...] = _layer_norm(alpha * x1_ref[...] + (1.0 + gate2) * acc_ref[...],
                                 g_ref[...], be_ref[...])


def _peer(h2, u, v, gm, x1, mod, ln_g, ln_b, *, seq, alpha, tb=512, eb=1024):
    t, d = h2.shape
    ne = u.shape[0]
    tb = min(tb, seq)
    per_b = seq // tb
    row = lambda i, j: (i, 0)
    const = lambda i, j: (0, 0)
    kernel = functools.partial(_peer_kernel, alpha=alpha)
    return pl.pallas_call(
        kernel,
        grid=(t // tb, ne // eb),
        in_specs=[pl.BlockSpec((tb, d), row),
                  pl.BlockSpec((eb, d), lambda i, j: (j, 0)),
                  pl.BlockSpec((eb, d), lambda i, j: (j, 0)),
                  pl.BlockSpec((tb, eb), lambda i, j: (i, j)),
                  pl.BlockSpec((tb, d), row),
                  pl.BlockSpec((None, 6, d), lambda i, j: (i // per_b, 0, 0)),
                  pl.BlockSpec((1, d), const), pl.BlockSpec((1, d), const)],
        out_specs=pl.BlockSpec((tb, d), row),
        out_shape=jax.ShapeDtypeStruct((t, d), F32),
        scratch_shapes=[pltpu.VMEM((tb, d), F32)],
        compiler_params=_params(("arbitrary", "arbitrary"), 56),
        name="peer",
    )(h2, u, v, gm, x1, mod, ln_g, ln_b)


def kernel(x, c, w_ada, b_ada, w_in, hg_lb_logits, hg_norm_g, w_branch_a, w_branch_b, w_out,
           ln1_g, ln1_b, peer_wq, peer_subkeys, peer_u, peer_v, ln2_g, ln2_b):
    batch, seq, d = x.shape
    depth = w_ada.shape[0]
    alpha = (2.0 * depth) ** 0.25
    lower_bounds = jnp.cumsum(jax.nn.softmax(hg_lb_logits.astype(F32), axis=0), axis=0)

    r = lax.broadcasted_iota(jnp.int32, (LANES, LANES), 0)
    cc = lax.broadcasted_iota(jnp.int32, (LANES, LANES), 1)
    rb = lax.broadcasted_iota(jnp.int32, (SB_KBLOCK, SB_KBLOCK), 0)
    cb = lax.broadcasted_iota(jnp.int32, (SB_KBLOCK, SB_KBLOCK), 1)
    tri = jnp.concatenate([(rb > cb), jnp.ones((SB_KBLOCK, LANES), bool)], axis=1).astype(BF16)
    ltri = (r[:HG_CHUNK, :HG_CHUNK] >= cc[:HG_CHUNK, :HG_CHUNK]).astype(BF16)
    ones = jnp.ones((LANES, LANES), BF16)

    c_pad = jnp.zeros((8, d), F32).at[:batch].set(c)
    xt = x.reshape(batch * seq, d)
    for layer in range(depth):
        mod = _ada(c_pad, w_ada[layer:layer + 1], b_ada[layer:layer + 1])[:batch].reshape(batch, 6, d)
        proj = _inproj(xt, mod, w_in[layer].astype(BF16), seq=seq)
        a_out = _sbattn(proj, tri, batch=batch, seq=seq)
        b_out = _hgrn(proj, lower_bounds[layer].reshape(HG_HEADS, 1, LANES),
                      hg_norm_g[layer].reshape(HG_HEADS, 1, LANES), ltri, ones, batch=batch, seq=seq)
        x1, h2 = _mix(a_out, b_out, proj, xt, mod, w_branch_a[layer].astype(BF16),
                      w_branch_b[layer].astype(BF16), w_out[layer].astype(BF16),
                      ln1_g[layer:layer + 1], ln1_b[layer:layer + 1], seq=seq, alpha=alpha)
        keys = peer_subkeys[layer].reshape(PEER_HEADS * 2, N_KEYS, LANES).astype(BF16)
        ei, ej, gt = _route(h2, peer_wq[layer].astype(BF16), keys)
        gm = _scatter(ei, ej, gt)
        xt = _peer(h2, peer_u[layer].astype(BF16), peer_v[layer].astype(BF16), gm, x1, mod,
                   ln2_g[layer:layer + 1], ln2_b[layer:layer + 1], seq=seq, alpha=alpha)
    return xt.reshape(batch, seq, d)
```

```python
import functools
import math

import jax
import jax.numpy as jnp
from jax import lax
from jax.experimental import pallas as pl
from jax.experimental.pallas import tpu as pltpu

F32 = jnp.float32
BF16 = jnp.bfloat16

LANES = 128
SB_HEADS = 8
SB_QBLOCK = 512
SB_KBLOCK = 256
HG_HEADS = 8
HG_CHUNK = 64
HG_SUB = 16
LOG2E = 1.4426950408889634
PEER_HEADS = 8
PEER_TOPK = 16
N_KEYS = 128
RMS_EPS = 1e-6
LN_EPS = 1e-5
MIB = 1024 * 1024

_NT = (((1,), (1,)), ((), ()))
_TN = (((0,), (0,)), ((), ()))


def _params(semantics, vmem_mib):
    return pltpu.CompilerParams(dimension_semantics=semantics, vmem_limit_bytes=vmem_mib * MIB)


def _ada_kernel(c_ref, w_ref, b_ref, o_ref):
    c = c_ref[...]
    cond = c * jax.nn.sigmoid(c)
    o_ref[...] = jnp.dot(cond.astype(BF16), w_ref[...].astype(BF16),
                         preferred_element_type=F32) + b_ref[...]


def _ada(c_pad, w_ada, b_ada, *, tn=1024):
    rows, d = c_pad.shape
    n = w_ada.shape[-1]
    return pl.pallas_call(
        _ada_kernel,
        grid=(n // tn,),
        in_specs=[pl.BlockSpec((rows, d), lambda j: (0, 0)),
                  pl.BlockSpec((None, d, tn), lambda j: (0, 0, j)),
                  pl.BlockSpec((1, tn), lambda j: (0, j))],
        out_specs=pl.BlockSpec((rows, tn), lambda j: (0, j)),
        out_shape=jax.ShapeDtypeStruct((rows, n), F32),
        compiler_params=_params(("arbitrary",), 40),
        name="ada",
    )(c_pad, w_ada, b_ada)


def _inproj_kernel(x_ref, mod_ref, w_ref, o_ref, h_scr):
    @pl.when(pl.program_id(1) == 0)
    def _():
        shift = mod_ref[0:1, :]
        scale = mod_ref[1:2, :]
        h_scr[...] = (x_ref[...] * (1.0 + scale) + shift).astype(BF16)

    o_ref[...] = jnp.dot(h_scr[...], w_ref[...], preferred_element_type=F32).astype(BF16)


def _inproj(x2, mod, w_in, *, seq, tm=1024, tn=1024):
    t, d = x2.shape
    n = w_in.shape[1]
    tm = min(tm, seq)
    per_b = seq // tm
    return pl.pallas_call(
        _inproj_kernel,
        grid=(t // tm, n // tn),
        in_specs=[pl.BlockSpec((tm, d), lambda i, j: (i, 0)),
                  pl.BlockSpec((None, 6, d), lambda i, j: (i // per_b, 0, 0)),
                  pl.BlockSpec((d, tn), lambda i, j: (0, j))],
        out_specs=pl.BlockSpec((tm, tn), lambda i, j: (i, j)),
        out_shape=jax.ShapeDtypeStruct((t, n), BF16),
        scratch_shapes=[pltpu.VMEM((tm, d), BF16)],
        compiler_params=_params(("arbitrary", "arbitrary"), 48),
        name="inproj",
    )(x2, mod, w_in)


def _sb_kernel(q_ref, k_ref, v_ref, tri_ref, o_ref, *, tq, tk, scale):
    qi = pl.program_id(1)
    nd = tq // tk
    q = q_ref[...]
    tri = tri_ref[...]
    row = lax.broadcasted_iota(jnp.int32, (tq, tk), 0)
    col = lax.broadcasted_iota(jnp.int32, (tq, tk), 1)

    def tile(j, acc, out, mask):
        ks = pl.multiple_of(j * tk, tk)
        zr = lax.dot_general(q, k_ref[pl.ds(ks, tk), :], _NT, preferred_element_type=F32)
        z = zr * scale
        lp = jnp.log(1.0 + jnp.exp2(jnp.abs(zr) * (-scale * LOG2E)))
        logb = jnp.minimum(z, 0.0) - lp
        l1mb = logb - z
        if mask is not None:
            l1mb = jnp.where(mask, l1mb, 0.0)
            logb = jnp.where(mask, logb, -1e30)
        cs = jnp.dot(l1mb.astype(BF16), tri, preferred_element_type=F32)
        w = jnp.exp(logb + cs[:, :tk] + jnp.concatenate([acc] * (tk // LANES), axis=1))
        out = out + jnp.dot(w.astype(BF16), v_ref[pl.ds(ks, tk), :], preferred_element_type=F32)
        return acc + cs[:, tk:], out

    acc = jnp.zeros((tq, LANES), F32)
    out = jnp.zeros((tq, LANES), F32)
    for dd in reversed(range(nd)):
        acc, out = tile(qi * nd + dd, acc, out, (col + dd * tk) < row)

    def steps(j, n, carry):
        a, o = carry
        for u in range(n):
            a, o = tile(j - u, a, o, None)
        return a, o

    pairs = qi // 2
    acc, out = lax.fori_loop(
        0, pairs, lambda i, c: steps((qi - 2 * i) * nd - 1, 2 * nd, c), (acc, out))
    acc, out = lax.fori_loop(
        0, qi - 2 * pairs, lambda i, c: steps(nd - 1, nd, c), (acc, out))
    o_ref[...] = out.astype(o_ref.dtype)


def _sbattn(proj, tri, *, batch, seq, tq=SB_QBLOCK, tk=SB_KBLOCK):
    t = proj.shape[0]
    tq = min(tq, seq)
    nq = seq // tq
    kernel = functools.partial(_sb_kernel, tq=tq, tk=tk, scale=1.0 / math.sqrt(LANES))
    return pl.pallas_call(
        kernel,
        grid=(batch * SB_HEADS, nq),
        in_specs=[pl.BlockSpec((tq, LANES), lambda bh, qi: ((bh // SB_HEADS) * nq + qi, bh % SB_HEADS)),
                  pl.BlockSpec((seq, LANES), lambda bh, qi: (bh // SB_HEADS, SB_HEADS + bh % SB_HEADS)),
                  pl.BlockSpec((seq, LANES), lambda bh, qi: (bh // SB_HEADS, 2 * SB_HEADS + bh % SB_HEADS)),
                  pl.BlockSpec((tk, tk + LANES), lambda bh, qi: (0, 0))],
        out_specs=pl.BlockSpec((tq, LANES), lambda bh, qi: ((bh // SB_HEADS) * nq + qi, bh % SB_HEADS)),
        out_shape=jax.ShapeDtypeStruct((t, SB_HEADS * LANES), BF16),
        compiler_params=_params(("arbitrary", "arbitrary"), 40),
        name="sbattn",
    )(proj, proj, proj, tri)


def _hgrn_kernel(q_ref, f_ref, i_ref, g_ref, lb_ref, ng_ref, ltri_ref, ones_ref, o_ref, st_ref,
                 *, rb, c, sb):
    @pl.when(pl.program_id(1) == 0)
    def _():
        st_ref[...] = jnp.zeros_like(st_ref)

    lb = lb_ref[...]
    ng = ng_ref[...]
    ltri = ltri_ref[...]
    ones = ones_ref[...]
    lane8 = lax.broadcasted_iota(jnp.int32, (8, c), 1)
    sub8 = lax.broadcasted_iota(jnp.int32, (8, LANES), 0)

    pre = []
    for ci in range(rb // c):
        r0 = ci * c
        qf = q_ref[r0:r0 + c, :].astype(F32)
        ff = f_ref[r0:r0 + c, :].astype(F32)
        vb = i_ref[r0:r0 + c, :]
        fg = lb + (1.0 - lb) * jax.nn.sigmoid(ff)
        lf = jnp.log(fg)
        kk = 1.0 - fg
        hi = lf.astype(BF16)
        lo = (lf - hi.astype(F32)).astype(BF16)
        b = (jnp.dot(ltri, hi, preferred_element_type=F32)
             + jnp.dot(ltri, lo, preferred_element_type=F32))

        b2 = b * LOG2E

        groups = [jnp.zeros((8, c), F32), jnp.zeros((8, c), F32)]
        for i in range(1, c // sb):
            r = b2[i * sb - 1:i * sb, :]
            qs = (qf[i * sb:(i + 1) * sb, :] * jnp.exp2(b2[i * sb:(i + 1) * sb, :] - r)).astype(BF16)
            ks = (kk[:i * sb, :] * jnp.exp2(r - b2[:i * sb, :])).astype(BF16)
            ks = jnp.concatenate([ks, jnp.zeros((c - i * sb, LANES), BF16)], axis=0)
            off_diag = lax.dot_general(qs, ks, _NT, preferred_element_type=F32)
            groups.extend([off_diag[:8, :], off_diag[8:, :]])

        slabs = []
        for s in range(c):
            g0, end = (s // 8) * 8, (s // sb + 1) * sb
            p = qf[g0:end, :] * kk[s:s + 1, :] * jnp.exp2(b2[g0:end, :] - b2[s:s + 1, :])
            if s % 8:
                head = jnp.where(sub8 >= (s % 8), p[:8, :], 0.0)
                p = head if end - g0 == 8 else jnp.concatenate([head, p[8:, :]], axis=0)
            slabs.append(p)
        red = jnp.dot(jnp.concatenate(slabs, axis=0).astype(BF16), ones, preferred_element_type=F32)
        off = 0
        for s in range(c):
            for g in range(s // 8, (s // sb + 1) * (sb // 8)):
                groups[g] = jnp.where(lane8 == s, red[off:off + 8, :c], groups[g])
                off += 8
        scores = jnp.concatenate(groups, axis=0)

        o_intra = jnp.dot(scores.astype(BF16), vb, preferred_element_type=F32)
        b_last = b2[c - 1:c, :]
        qt = (qf * jnp.exp2(b2)).astype(BF16)
        kt = (kk * jnp.exp2(b_last - b2)).astype(BF16)
        upd = lax.dot_general(vb, kt, _TN, preferred_element_type=F32)
        pre.append((o_intra, qt, upd, jnp.exp2(b_last)))

    st = st_ref[...]
    for ci, (o_intra, qt, upd, dec) in enumerate(pre):
        r0 = ci * c
        o = o_intra + lax.dot_general(qt, st.astype(BF16), _NT, preferred_element_type=F32)
        st = st * dec + upd
        o = o * lax.rsqrt(jnp.mean(o * o, axis=-1, keepdims=True) + RMS_EPS)
        gg = g_ref[r0:r0 + c, :].astype(F32)
        o_ref[r0:r0 + c, :] = (o * ng * (gg * jax.nn.sigmoid(gg))).astype(o_ref.dtype)
    st_ref[...] = st


def _hgrn(proj, lb, ng, ltri, ones, *, batch, seq, rb=1024):
    t = proj.shape[0]
    rb = min(rb, seq)
    nr = seq // rb
    base = 3 * SB_HEADS

    def col(k):
        return pl.BlockSpec((rb, LANES),
                            lambda bh, r: ((bh // HG_HEADS) * nr + r, base + k * HG_HEADS + bh % HG_HEADS))

    vec = pl.BlockSpec((None, 1, LANES), lambda bh, r: (bh % HG_HEADS, 0, 0))
    kernel = functools.partial(_hgrn_kernel, rb=rb, c=HG_CHUNK, sb=HG_SUB)
    return pl.pallas_call(
        kernel,
        grid=(batch * HG_HEADS, nr),
        in_specs=[col(0), col(1), col(2), col(3), vec, vec,
                  pl.BlockSpec((HG_CHUNK, HG_CHUNK), lambda bh, r: (0, 0)),
                  pl.BlockSpec((LANES, LANES), lambda bh, r: (0, 0))],
        out_specs=pl.BlockSpec((rb, LANES), lambda bh, r: ((bh // HG_HEADS) * nr + r, bh % HG_HEADS)),
        out_shape=jax.ShapeDtypeStruct((t, HG_HEADS * LANES), BF16),
        scratch_shapes=[pltpu.VMEM((LANES, LANES), F32)],
        compiler_params=_params(("arbitrary", "arbitrary"), 32),
        name="hgrn",
    )(proj, proj, proj, proj, lb, ng, ltri, ones)


def _layer_norm(z, g, b):
    mu = jnp.mean(z, axis=-1, keepdims=True)
    zc = z - mu
    var = jnp.mean(zc * zc, axis=-1, keepdims=True)
    return zc * lax.rsqrt(var + LN_EPS) * g + b


def _mix_kernel(a_ref, b_ref, ga0_ref, ga1_ref, gb0_ref, gb1_ref, x_ref, mod_ref, wa_ref, wb_ref,
                wo_ref, g_ref, be_ref, x1_ref, h2_ref, *, alpha):
    a = a_ref[...]
    b = b_ref[...]
    half = wa_ref.shape[1] // 2
    merged = []
    for n, (ga_ref, gb_ref) in enumerate(((ga0_ref, gb0_ref), (ga1_ref, gb1_ref))):
        pa = jnp.dot(a, wa_ref[:, n * half:(n + 1) * half], preferred_element_type=F32)
        pb = jnp.dot(b, wb_ref[:, n * half:(n + 1) * half], preferred_element_type=F32)
        m = (jax.nn.sigmoid(ga_ref[...].astype(F32)) * pa
             + jax.nn.sigmoid(gb_ref[...].astype(F32)) * pb)
        merged.append(m.astype(BF16))
    y = jnp.dot(jnp.concatenate(merged, axis=1), wo_ref[...], preferred_element_type=F32)
    gate1 = mod_ref[2:3, :]
    shift2 = mod_ref[3:4, :]
    scale2 = mod_ref[4:5, :]
    x1 = _layer_norm(alpha * x_ref[...] + (1.0 + gate1) * y, g_ref[...], be_ref[...])
    x1_ref[...] = x1
    h2_ref[...] = (x1 * (1.0 + scale2) + shift2).astype(BF16)


def _mix(a_out, b_out, proj, x2, mod, wa, wb, wo, ln_g, ln_b, *, seq, alpha, tm=512):
    t, d = x2.shape
    w = a_out.shape[1]
    tm = min(tm, seq)
    per_b = seq // tm
    gbase = (3 * SB_HEADS + 4 * HG_HEADS) * LANES // w
    row = lambda i: (i, 0)
    const = lambda i: (0, 0)

    def gate(k):
        return pl.BlockSpec((tm, w), lambda i: (i, gbase + k))

    kernel = functools.partial(_mix_kernel, alpha=alpha)
    return pl.pallas_call(
        kernel,
        grid=(t // tm,),
        in_specs=[pl.BlockSpec((tm, w), row), pl.BlockSpec((tm, w), row),
                  gate(0), gate(1), gate(2), gate(3),
                  pl.BlockSpec((tm, d), row),
                  pl.BlockSpec((None, 6, d), lambda i: (i // per_b, 0, 0)),
                  pl.BlockSpec((w, d), const, pipeline_mode=pl.Buffered(1)),
                  pl.BlockSpec((w, d), const, pipeline_mode=pl.Buffered(1)),
                  pl.BlockSpec((d, d), const, pipeline_mode=pl.Buffered(1)),
                  pl.BlockSpec((1, d), const), pl.BlockSpec((1, d), const)],
        out_specs=[pl.BlockSpec((tm, d), row), pl.BlockSpec((tm, d), row)],
        out_shape=[jax.ShapeDtypeStruct((t, d), F32), jax.ShapeDtypeStruct((t, d), BF16)],
        compiler_params=_params(("arbitrary",), 56),
        name="mix",
    )(a_out, b_out, proj, proj, proj, proj, x2, mod, wa, wb, wo, ln_g, ln_b)


def _top_rows(vals, payloads, k, n_rows):
    rid = lax.broadcasted_iota(jnp.int32, vals.shape, 0).astype(F32)
    best, rows, outs = [], [], [[] for _ in payloads]
    for _ in range(k):
        m = jnp.max(vals, axis=0, keepdims=True)
        first = jnp.min(jnp.where(vals == m, rid, float(n_rows)), axis=0, keepdims=True)
        sel = rid == first
        best.append(m)
        rows.append(first)
        for o, p in zip(outs, payloads):
            o.append(jnp.sum(jnp.where(sel, p, 0.0), axis=0, keepdims=True))
        vals = jnp.where(sel, -jnp.inf, vals)
    return best, rows, outs


def _route_kernel(h_ref, wq_ref, keys_ref, ei_ref, ej_ref, gt_ref):
    tb = h_ref.shape[0]
    k = PEER_TOPK
    q = jnp.dot(h_ref[...], wq_ref[...], preferred_element_type=F32).astype(BF16)
    sub8 = lax.broadcasted_iota(jnp.int32, (8, tb), 0)
    ei, ej, gt = [], [], []
    for h in range(PEER_HEADS):
        tops, idxs = [], []
        for p in range(2):
            c0 = (h * 2 + p) * LANES
            s = lax.dot_general(keys_ref[h * 2 + p], q[:, c0:c0 + LANES], _NT,
                                preferred_element_type=F32)
            best, rows, _ = _top_rows(s, [], k, N_KEYS)
            tops.append(best)
            idxs.append(rows)
        top1 = jnp.concatenate(tops[1], axis=0)
        idx1 = jnp.concatenate(idxs[1], axis=0)
        cand = [tops[0][0] + top1]
        ci = [jnp.broadcast_to(idxs[0][0], (k, tb))]
        cj = [idx1]
        for a in range(1, 8):
            ca = tops[0][a] + top1[:8]
            if k // (a + 1) < 8:
                ca = jnp.where(sub8 < k // (a + 1), ca, -jnp.inf)
            cand.append(ca)
            ci.append(jnp.broadcast_to(idxs[0][a], (8, tb)))
            cj.append(idx1[:8])
        cand.append(jnp.concatenate(tops[0][8:], axis=0) + top1[:1])
        ci.append(jnp.concatenate(idxs[0][8:], axis=0))
        cj.append(jnp.broadcast_to(idx1[:1], (8, tb)))
        cand = jnp.concatenate(cand, axis=0)
        best, _, (bi, bj) = _top_rows(cand, [jnp.concatenate(ci, axis=0), jnp.concatenate(cj, axis=0)],
                                      k, cand.shape[0])
        e = [jnp.exp(v - best[0]) for v in best]
        denom = functools.reduce(lambda a, b: a + b, e)
        inv = 1.0 / denom
        ei.extend(bi)
        ej.extend(bj)
        gt.extend([v * inv for v in e])
    ei_ref[...] = jnp.concatenate(ei, axis=0).T
    ej_ref[...] = jnp.concatenate(ej, axis=0).T
    gt_ref[...] = jnp.concatenate(gt, axis=0).T


def _route(h2, wq, keys, *, tb=256):
    t, d = h2.shape
    tb = min(tb, t)
    nk = PEER_HEADS * PEER_TOPK
    out = jax.ShapeDtypeStruct((t, nk), F32)
    ospec = pl.BlockSpec((tb, nk), lambda i: (i, 0))
    return pl.pallas_call(
        _route_kernel,
        grid=(t // tb,),
        in_specs=[pl.BlockSpec((tb, d), lambda i: (i, 0)),
                  pl.BlockSpec(wq.shape, lambda i: (0, 0)),
                  pl.BlockSpec(keys.shape, lambda i: (0, 0, 0))],
        out_specs=[ospec, ospec, ospec],
        out_shape=[out, out, out],
        compiler_params=_params(("arbitrary",), 48),
        name="route",
    )(h2, wq, keys)


def _scatter_kernel(ei_ref, ej_ref, gt_ref, o_ref, *, tb, grp):
    sub = lax.broadcasted_iota(jnp.int32, (N_KEYS, LANES), 0).astype(F32).astype(BF16)
    one = jnp.ones((), BF16)
    zero = jnp.zeros((), BF16)

    def body(g, carry):
        t0 = pl.multiple_of(g * grp, grp)
        prods = []
        for u in range(grp):
            irow = ei_ref[pl.ds(t0 + u, 1), :].astype(BF16)
            jrow = ej_ref[pl.ds(t0 + u, 1), :].astype(BF16)
            grow = gt_ref[pl.ds(t0 + u, 1), :].astype(BF16)
            r = jnp.where(sub == irow, grow, zero)
            cm = jnp.where(sub == jrow, one, zero)
            prods.append(lax.dot_general(r, cm, _NT, preferred_element_type=F32))
        y = jnp.swapaxes(jnp.stack(prods, axis=0), 0, 1).astype(o_ref.dtype)
        for i in range(N_KEYS):
            o_ref[pl.ds(t0, grp), i * N_KEYS:(i + 1) * N_KEYS] = y[i]
        return carry

    lax.fori_loop(0, tb // grp, body, 0)


def _scatter(ei, ej, gt, *, tb=256, grp=16):
    t, nk = ei.shape
    tb = min(tb, t)
    spec = pl.BlockSpec((tb, nk), lambda i: (i, 0))
    kernel = functools.partial(_scatter_kernel, tb=tb, grp=grp)
    return pl.pallas_call(
        kernel,
        grid=(t // tb,),
        in_specs=[spec, spec, spec],
        out_specs=pl.BlockSpec((tb, N_KEYS * N_KEYS), lambda i: (i, 0)),
        out_shape=jax.ShapeDtypeStruct((t, N_KEYS * N_KEYS), BF16),
        compiler_params=_params(("arbitrary",), 40),
        name="scatter",
    )(ei, ej, gt)


def _peer_kernel(h_ref, u_ref, v_ref, gm_ref, x1_ref, mod_ref, g_ref, be_ref, o_ref, w_ref, *, alpha, nj):
    s = pl.program_id(0)
    jp = (s + nj - 1) % nj

    @pl.when(jnp.logical_or(s == 0, jp == 0))
    def _():
        o_ref[...] = jnp.zeros_like(o_ref)

    @pl.when(s == 0)
    def _():
        w_ref[...] = jnp.zeros_like(w_ref)

    w_prev = w_ref[...]
    act = lax.dot_general(h_ref[...], u_ref[...], _NT, preferred_element_type=F32)
    o_ref[...] += jnp.dot(w_prev, v_ref[...], preferred_element_type=F32)
    gelu = 0.5 * act * (1.0 + lax.erf(act * (1.0 / math.sqrt(2.0))))
    w_ref[...] = (gm_ref[...].astype(F32) * gelu).astype(BF16)

    @pl.when(jnp.logical_and(s > 0, jp == nj - 1))
    def _():
        gate2 = mod_ref[5:6, :]
        o_ref[...] = _layer_norm(alpha * x1_ref[...] + (1.0 + gate2) * o_ref[...],
                                 g_ref[...], be_ref[...])


def _peer(h2, u, v, gm, x1, mod, ln_g, ln_b, *, seq, alpha, tb=512, eb=1024):
    t, d = h2.shape
    ne = u.shape[0]
    tb = min(tb, seq)
    per_b = seq // tb
    nj = ne // eb
    last = (t // tb) * nj - 1
    cur = lambda s: jnp.minimum(s, last)
    prev = lambda s: jnp.maximum(s - 1, 0)
    const = lambda s: (0, 0)
    kernel = functools.partial(_peer_kernel, alpha=alpha, nj=nj)
    return pl.pallas_call(
        kernel,
        grid=(last + 2,),
        in_specs=[pl.BlockSpec((tb, d), lambda s: (cur(s) // nj, 0)),
                  pl.BlockSpec((eb, d), lambda s: (cur(s) % nj, 0)),
                  pl.BlockSpec((eb, d), lambda s: (prev(s) % nj, 0)),
                  pl.BlockSpec((tb, eb), lambda s: (cur(s) // nj, cur(s) % nj)),
                  pl.BlockSpec((tb, d), lambda s: (prev(s) // nj, 0)),
                  pl.BlockSpec((None, 6, d), lambda s: (prev(s) // nj // per_b, 0, 0)),
                  pl.BlockSpec((1, d), const), pl.BlockSpec((1, d), const)],
        out_specs=pl.BlockSpec((tb, d), lambda s: (prev(s) // nj, 0)),
        out_shape=jax.ShapeDtypeStruct((t, d), F32),
        scratch_shapes=[pltpu.VMEM((tb, eb), BF16)],
        compiler_params=_params(("arbitrary",), 56),
        name="peer",
    )(h2, u, v, gm, x1, mod, ln_g, ln_b)


def kernel(x, c, w_ada, b_ada, w_in, hg_lb_logits, hg_norm_g, w_branch_a, w_branch_b, w_out,
           ln1_g, ln1_b, peer_wq, peer_subkeys, peer_u, peer_v, ln2_g, ln2_b):
    batch, seq, d = x.shape
    depth = w_ada.shape[0]
    alpha = (2.0 * depth) ** 0.25
    lower_bounds = jnp.cumsum(jax.nn.softmax(hg_lb_logits.astype(F32), axis=0), axis=0)

    r = lax.broadcasted_iota(jnp.int32, (LANES, LANES), 0)
    cc = lax.broadcasted_iota(jnp.int32, (LANES, LANES), 1)
    rb = lax.broadcasted_iota(jnp.int32, (SB_KBLOCK, SB_KBLOCK), 0)
    cb = lax.broadcasted_iota(jnp.int32, (SB_KBLOCK, SB_KBLOCK), 1)
    tri = jnp.concatenate([(rb > cb), jnp.ones((SB_KBLOCK, LANES), bool)], axis=1).astype(BF16)
    ltri = (r[:HG_CHUNK, :HG_CHUNK] >= cc[:HG_CHUNK, :HG_CHUNK]).astype(BF16)
    ones = jnp.ones((LANES, LANES), BF16)

    c_pad = jnp.zeros((8, d), F32).at[:batch].set(c)
    xt = x.reshape(batch * seq, d)
    for layer in range(depth):
        mod = _ada(c_pad, w_ada[layer:layer + 1], b_ada[layer:layer + 1])[:batch].reshape(batch, 6, d)
        proj = _inproj(xt, mod, w_in[layer].astype(BF16), seq=seq)
        a_out = _sbattn(proj, tri, batch=batch, seq=seq)
        b_out = _hgrn(proj, lower_bounds[layer].reshape(HG_HEADS, 1, LANES),
                      hg_norm_g[layer].reshape(HG_HEADS, 1, LANES), ltri, ones, batch=batch, seq=seq)
        x1, h2 = _mix(a_out, b_out, proj, xt, mod, w_branch_a[layer].astype(BF16),
                      w_branch_b[layer].astype(BF16), w_out[layer].astype(BF16),
                      ln1_g[layer:layer + 1], ln1_b[layer:layer + 1], seq=seq, alpha=alpha)
        keys = peer_subkeys[layer].reshape(PEER_HEADS * 2, N_KEYS, LANES).astype(BF16)
        ei, ej, gt = _route(h2, peer_wq[layer].astype(BF16), keys)
        gm = _scatter(ei, ej, gt)
        xt = _peer(h2, peer_u[layer].astype(BF16), peer_v[layer].astype(BF16), gm, x1, mod,
                   ln2_g[layer:layer + 1], ln2_b[layer:layer + 1], seq=seq, alpha=alpha)
    return xt.reshape(batch, seq, d)
```

```python
import functools
import math

import jax
import jax.numpy as jnp
from jax import lax
from jax.experimental import pallas as pl
from jax.experimental.pallas import tpu as pltpu

F32 = jnp.float32
BF16 = jnp.bfloat16

LANES = 128
SB_HEADS = 8
SB_QBLOCK = 512
SB_KBLOCK = 256
HG_HEADS = 8
HG_CHUNK = 64
HG_SUB = 16
LOG2E = 1.4426950408889634
PEER_HEADS = 8
PEER_TOPK = 16
N_KEYS = 128
RMS_EPS = 1e-6
LN_EPS = 1e-5
MIB = 1024 * 1024

_NT = (((1,), (1,)), ((), ()))
_TN = (((0,), (0,)), ((), ()))


def _params(semantics, vmem_mib):
    return pltpu.CompilerParams(dimension_semantics=semantics, vmem_limit_bytes=vmem_mib * MIB)


def _ada_kernel(c_ref, w_ref, b_ref, o_ref):
    c = c_ref[...]
    cond = c * jax.nn.sigmoid(c)
    o_ref[...] = jnp.dot(cond.astype(BF16), w_ref[...].astype(BF16),
                         preferred_element_type=F32) + b_ref[...]


def _ada(c_pad, w_ada, b_ada, *, tn=1024):
    rows, d = c_pad.shape
    n = w_ada.shape[-1]
    return pl.pallas_call(
        _ada_kernel,
        grid=(n // tn,),
        in_specs=[pl.BlockSpec((rows, d), lambda j: (0, 0)),
                  pl.BlockSpec((None, d, tn), lambda j: (0, 0, j)),
                  pl.BlockSpec((1, tn), lambda j: (0, j))],
        out_specs=pl.BlockSpec((rows, tn), lambda j: (0, j)),
        out_shape=jax.ShapeDtypeStruct((rows, n), F32),
        compiler_params=_params(("arbitrary",), 40),
        name="ada",
    )(c_pad, w_ada, b_ada)


def _inproj_kernel(x_ref, mod_ref, w_ref, o_ref, h_scr):
    @pl.when(pl.program_id(1) == 0)
    def _():
        shift = mod_ref[0:1, :]
        scale = mod_ref[1:2, :]
        h_scr[...] = (x_ref[...] * (1.0 + scale) + shift).astype(BF16)

    o_ref[...] = jnp.dot(h_scr[...], w_ref[...], preferred_element_type=F32).astype(BF16)


def _inproj(x2, mod, w_in, *, seq, tm=1024, tn=1024):
    t, d = x2.shape
    n = w_in.shape[1]
    tm = min(tm, seq)
    per_b = seq // tm
    return pl.pallas_call(
        _inproj_kernel,
        grid=(t // tm, n // tn),
        in_specs=[pl.BlockSpec((tm, d), lambda i, j: (i, 0)),
                  pl.BlockSpec((None, 6, d), lambda i, j: (i // per_b, 0, 0)),
                  pl.BlockSpec((d, tn), lambda i, j: (0, j))],
        out_specs=pl.BlockSpec((tm, tn), lambda i, j: (i, j)),
        out_shape=jax.ShapeDtypeStruct((t, n), BF16),
        scratch_shapes=[pltpu.VMEM((tm, d), BF16)],
        compiler_params=_params(("arbitrary", "arbitrary"), 48),
        name="inproj",
    )(x2, mod, w_in)


def _sb_kernel(q_ref, k_ref, v_ref, tri_ref, o_ref, *, tq, tk, scale):
    qi = pl.program_id(1)
    nd = tq // tk
    q = q_ref[...]
    tri = tri_ref[...]

    def tile(j, acc, out, r0=None):
        ks = pl.multiple_of(j * tk, tk)
        zr = lax.dot_general(q[r0:], k_ref[pl.ds(ks, tk), :], _NT, preferred_element_type=F32)
        z = zr * scale
        lp = jnp.log(1.0 + jnp.exp2(jnp.abs(zr) * (-scale * LOG2E)))
        logb = jnp.minimum(z, 0.0) - lp
        l1mb = logb - z
        if r0 is not None:
            mask = (lax.broadcasted_iota(jnp.int32, zr.shape, 1)
                    < lax.broadcasted_iota(jnp.int32, zr.shape, 0))
            l1mb = jnp.where(mask, l1mb, 0.0)
            logb = jnp.where(mask, logb, -1e30)
        cs = jnp.dot(l1mb.astype(BF16), tri, preferred_element_type=F32)
        w = jnp.exp(logb + cs[:, :tk] + jnp.concatenate([acc[r0:]] * (tk // LANES), axis=1))
        pv = jnp.dot(w.astype(BF16), v_ref[pl.ds(ks, tk), :], preferred_element_type=F32)
        if r0:
            pad = jnp.zeros((r0, LANES), F32)
            return acc + jnp.concatenate([pad, cs[:, tk:]], axis=0), out + jnp.concatenate([pad, pv], axis=0)
        return acc + cs[:, tk:], out + pv

    acc = jnp.zeros((tq, LANES), F32)
    out = jnp.zeros((tq, LANES), F32)
    for dd in reversed(range(nd)):
        acc, out = tile(qi * nd + dd, acc, out, dd * tk)

    def steps(j, n, carry):
        a, o = carry
        for u in range(n):
            a, o = tile(j - u, a, o, None)
        return a, o

    pairs = qi // 2
    acc, out = lax.fori_loop(
        0, pairs, lambda i, c: steps((qi - 2 * i) * nd - 1, 2 * nd, c), (acc, out))
    acc, out = lax.fori_loop(
        0, qi - 2 * pairs, lambda i, c: steps(nd - 1, nd, c), (acc, out))
    o_ref[...] = out.astype(o_ref.dtype)


def _sbattn(proj, tri, *, batch, seq, tq=SB_QBLOCK, tk=SB_KBLOCK):
    t = proj.shape[0]
    tq = min(tq, seq)
    nq = seq // tq
    kernel = functools.partial(_sb_kernel, tq=tq, tk=tk, scale=1.0 / math.sqrt(LANES))
    return pl.pallas_call(
        kernel,
        grid=(batch * SB_HEADS, nq),
        in_specs=[pl.BlockSpec((tq, LANES), lambda bh, qi: ((bh // SB_HEADS) * nq + qi, bh % SB_HEADS)),
                  pl.BlockSpec((seq, LANES), lambda bh, qi: (bh // SB_HEADS, SB_HEADS + bh % SB_HEADS)),
                  pl.BlockSpec((seq, LANES), lambda bh, qi: (bh // SB_HEADS, 2 * SB_HEADS + bh % SB_HEADS)),
                  pl.BlockSpec((tk, tk + LANES), lambda bh, qi: (0, 0))],
        out_specs=pl.BlockSpec((tq, LANES), lambda bh, qi: ((bh // SB_HEADS) * nq + qi, bh % SB_HEADS)),
        out_shape=jax.ShapeDtypeStruct((t, SB_HEADS * LANES), BF16),
        compiler_params=_params(("arbitrary", "arbitrary"), 40),
        name="sbattn",
    )(proj, proj, proj, tri)


def _hgrn_kernel(q_ref, f_ref, i_ref, g_ref, lb_ref, ng_ref, ltri_ref, ones_ref, o_ref, st_ref,
                 *, rb, c, sb):
    @pl.when(pl.program_id(1) == 0)
    def _():
        st_ref[...] = jnp.zeros_like(st_ref)

    lb = lb_ref[...]
    ng = ng_ref[...]
    ltri = ltri_ref[...]
    ones = ones_ref[...]
    lane8 = lax.broadcasted_iota(jnp.int32, (8, c), 1)
    sub8 = lax.broadcasted_iota(jnp.int32, (8, LANES), 0)

    pre = []
    for ci in range(rb // c):
        r0 = ci * c
        qf = q_ref[r0:r0 + c, :].astype(F32)
        ff = f_ref[r0:r0 + c, :].astype(F32)
        vb = i_ref[r0:r0 + c, :]
        fg = lb + (1.0 - lb) * jax.nn.sigmoid(ff)
        lf = jnp.log(fg)
        kk = 1.0 - fg
        hi = lf.astype(BF16)
        lo = (lf - hi.astype(F32)).astype(BF16)
        b = (jnp.dot(ltri, hi, preferred_element_type=F32)
             + jnp.dot(ltri, lo, preferred_element_type=F32))

        b2 = b * LOG2E

        groups = [jnp.zeros((8, c), F32), jnp.zeros((8, c), F32)]
        for i in range(1, c // sb):
            r = b2[i * sb - 1:i * sb, :]
            qs = (qf[i * sb:(i + 1) * sb, :] * jnp.exp2(b2[i * sb:(i + 1) * sb, :] - r)).astype(BF16)
            ks = (kk[:i * sb, :] * jnp.exp2(r - b2[:i * sb, :])).astype(BF16)
            ks = jnp.concatenate([ks, jnp.zeros((c - i * sb, LANES), BF16)], axis=0)
            off_diag = lax.dot_general(qs, ks, _NT, preferred_element_type=F32)
            groups.extend([off_diag[:8, :], off_diag[8:, :]])

        slabs = []
        for s in range(c):
            g0, end = (s // 8) * 8, (s // sb + 1) * sb
            p = qf[g0:end, :] * kk[s:s + 1, :] * jnp.exp2(b2[g0:end, :] - b2[s:s + 1, :])
            if s % 8:
                head = jnp.where(sub8 >= (s % 8), p[:8, :], 0.0)
                p = head if end - g0 == 8 else jnp.concatenate([head, p[8:, :]], axis=0)
            slabs.append(p)
        red = jnp.dot(jnp.concatenate(slabs, axis=0).astype(BF16), ones, preferred_element_type=F32)
        off = 0
        for s in range(c):
            for g in range(s // 8, (s // sb + 1) * (sb // 8)):
                groups[g] = jnp.where(lane8 == s, red[off:off + 8, :c], groups[g])
                off += 8
        scores = jnp.concatenate(groups, axis=0)

        o_intra = jnp.dot(scores.astype(BF16), vb, preferred_element_type=F32)
        b_last = b2[c - 1:c, :]
        qt = (qf * jnp.exp2(b2)).astype(BF16)
        kt = (kk * jnp.exp2(b_last - b2)).astype(BF16)
        upd = lax.dot_general(vb, kt, _TN, preferred_element_type=F32)
        pre.append((o_intra, qt, upd, jnp.exp2(b_last)))

    st = st_ref[...]
    for ci, (o_intra, qt, upd, dec) in enumerate(pre):
        r0 = ci * c
        o = o_intra + lax.dot_general(qt, st.astype(BF16), _NT, preferred_element_type=F32)
        st = st * dec + upd
        o = o * lax.rsqrt(jnp.mean(o * o, axis=-1, keepdims=True) + RMS_EPS)
        gg = g_ref[r0:r0 + c, :].astype(F32)
        o_ref[r0:r0 + c, :] = (o * ng * (gg * jax.nn.sigmoid(gg))).astype(o_ref.dtype)
    st_ref[...] = st


def _hgrn(proj, lb, ng, ltri, ones, *, batch, seq, rb=1024):
    t = proj.shape[0]
    rb = min(rb, seq)
    nr = seq // rb
    base = 3 * SB_HEADS

    def col(k):
        return pl.BlockSpec((rb, LANES),
                            lambda bh, r: ((bh // HG_HEADS) * nr + r, base + k * HG_HEADS + bh % HG_HEADS))

    vec = pl.BlockSpec((None, 1, LANES), lambda bh, r: (bh % HG_HEADS, 0, 0))
    kernel = functools.partial(_hgrn_kernel, rb=rb, c=HG_CHUNK, sb=HG_SUB)
    return pl.pallas_call(
        kernel,
        grid=(batch * HG_HEADS, nr),
        in_specs=[col(0), col(1), col(2), col(3), vec, vec,
                  pl.BlockSpec((HG_CHUNK, HG_CHUNK), lambda bh, r: (0, 0)),
                  pl.BlockSpec((LANES, LANES), lambda bh, r: (0, 0))],
        out_specs=pl.BlockSpec((rb, LANES), lambda bh, r: ((bh // HG_HEADS) * nr + r, bh % HG_HEADS)),
        out_shape=jax.ShapeDtypeStruct((t, HG_HEADS * LANES), BF16),
        scratch_shapes=[pltpu.VMEM((LANES, LANES), F32)],
        compiler_params=_params(("arbitrary", "arbitrary"), 32),
        name="hgrn",
    )(proj, proj, proj, proj, lb, ng, ltri, ones)


def _layer_norm(z, g, b):
    mu = jnp.mean(z, axis=-1, keepdims=True)
    zc = z - mu
    var = jnp.mean(zc * zc, axis=-1, keepdims=True)
    return zc * lax.rsqrt(var + LN_EPS) * g + b


def _mix_kernel(a_ref, b_ref, ga0_ref, ga1_ref, gb0_ref, gb1_ref, x_ref, mod_ref, wa_ref, wb_ref,
                wo_ref, g_ref, be_ref, x1_ref, h2_ref, *, alpha):
    a = a_ref[...]
    b = b_ref[...]
    half = wa_ref.shape[1] // 2
    merged = []
    for n, (ga_ref, gb_ref) in enumerate(((ga0_ref, gb0_ref), (ga1_ref, gb1_ref))):
        pa = jnp.dot(a, wa_ref[:, n * half:(n + 1) * half], preferred_element_type=F32)
        pb = jnp.dot(b, wb_ref[:, n * half:(n + 1) * half], preferred_element_type=F32)
        m = (jax.nn.sigmoid(ga_ref[...].astype(F32)) * pa
             + jax.nn.sigmoid(gb_ref[...].astype(F32)) * pb)
        merged.append(m.astype(BF16))
    y = jnp.dot(jnp.concatenate(merged, axis=1), wo_ref[...], preferred_element_type=F32)
    gate1 = mod_ref[2:3, :]
    shift2 = mod_ref[3:4, :]
    scale2 = mod_ref[4:5, :]
    x1 = _layer_norm(alpha * x_ref[...] + (1.0 + gate1) * y, g_ref[...], be_ref[...])
    x1_ref[...] = x1
    h2_ref[...] = (x1 * (1.0 + scale2) + shift2).astype(BF16)


def _mix(a_out, b_out, proj, x2, mod, wa, wb, wo, ln_g, ln_b, *, seq, alpha, tm=512):
    t, d = x2.shape
    w = a_out.shape[1]
    tm = min(tm, seq)
    per_b = seq // tm
    gbase = (3 * SB_HEADS + 4 * HG_HEADS) * LANES // w
    row = lambda i: (i, 0)
    const = lambda i: (0, 0)

    def gate(k):
        return pl.BlockSpec((tm, w), lambda i: (i, gbase + k))

    kernel = functools.partial(_mix_kernel, alpha=alpha)
    return pl.pallas_call(
        kernel,
        grid=(t // tm,),
        in_specs=[pl.BlockSpec((tm, w), row), pl.BlockSpec((tm, w), row),
                  gate(0), gate(1), gate(2), gate(3),
                  pl.BlockSpec((tm, d), row),
                  pl.BlockSpec((None, 6, d), lambda i: (i // per_b, 0, 0)),
                  pl.BlockSpec((w, d), const, pipeline_mode=pl.Buffered(1)),
                  pl.BlockSpec((w, d), const, pipeline_mode=pl.Buffered(1)),
                  pl.BlockSpec((d, d), const, pipeline_mode=pl.Buffered(1)),
                  pl.BlockSpec((1, d), const), pl.BlockSpec((1, d), const)],
        out_specs=[pl.BlockSpec((tm, d), row), pl.BlockSpec((tm, d), row)],
        out_shape=[jax.ShapeDtypeStruct((t, d), F32), jax.ShapeDtypeStruct((t, d), BF16)],
        compiler_params=_params(("arbitrary",), 56),
        name="mix",
    )(a_out, b_out, proj, proj, proj, proj, x2, mod, wa, wb, wo, ln_g, ln_b)


def _top_rows(vals, payloads, k, n_rows):
    rid = lax.broadcasted_iota(jnp.int32, vals.shape, 0).astype(F32)
    best, rows, outs = [], [], [[] for _ in payloads]
    for _ in range(k):
        m = jnp.max(vals, axis=0, keepdims=True)
        first = jnp.min(jnp.where(vals == m, rid, float(n_rows)), axis=0, keepdims=True)
        sel = rid == first
        best.append(m)
        rows.append(first)
        for o, p in zip(outs, payloads):
            o.append(jnp.sum(jnp.where(sel, p, 0.0), axis=0, keepdims=True))
        vals = jnp.where(sel, -jnp.inf, vals)
    return best, rows, outs


def _route_kernel(h_ref, wq_ref, keys_ref, ee_ref, gt_ref):
    tb = h_ref.shape[0]
    k = PEER_TOPK
    q = jnp.dot(h_ref[...], wq_ref[...], preferred_element_type=F32).astype(BF16)
    sub8 = lax.broadcasted_iota(jnp.int32, (8, tb), 0)
    ee, gt = [], []
    for h in range(PEER_HEADS):
        tops, idxs = [], []
        for p in range(2):
            c0 = (h * 2 + p) * LANES
            s = lax.dot_general(keys_ref[h * 2 + p], q[:, c0:c0 + LANES], _NT,
                                preferred_element_type=F32)
            best, rows, _ = _top_rows(s, [], k, N_KEYS)
            tops.append(best)
            idxs.append(rows)
        top1 = jnp.concatenate(tops[1], axis=0)
        idx1 = jnp.concatenate(idxs[1], axis=0)
        cand = [tops[0][0] + top1]
        ce = [idxs[0][0] * float(N_KEYS) + idx1]
        for a in range(1, 8):
            ca = tops[0][a] + top1[:8]
            if k // (a + 1) < 8:
                ca = jnp.where(sub8 < k // (a + 1), ca, -jnp.inf)
            cand.append(ca)
            ce.append(idxs[0][a] * float(N_KEYS) + idx1[:8])
        cand.append(jnp.concatenate(tops[0][8:], axis=0) + top1[:1])
        ce.append(jnp.concatenate(idxs[0][8:], axis=0) * float(N_KEYS) + idx1[:1])
        cand = jnp.concatenate(cand, axis=0)
        best, _, (be,) = _top_rows(cand, [jnp.concatenate(ce, axis=0)], k, cand.shape[0])
        e = [jnp.exp(v - best[0]) for v in best]
        denom = functools.reduce(lambda a, b: a + b, e)
        inv = 1.0 / denom
        ee.extend(be)
        gt.extend([v * inv for v in e])
    ee_ref[...] = jnp.concatenate(ee, axis=0).T
    gt_ref[...] = jnp.concatenate(gt, axis=0).T


def _route(h2, wq, keys, *, tb=256):
    t, d = h2.shape
    tb = min(tb, t)
    nk = PEER_HEADS * PEER_TOPK
    out = jax.ShapeDtypeStruct((t, nk), F32)
    ospec = pl.BlockSpec((tb, nk), lambda i: (i, 0))
    return pl.pallas_call(
        _route_kernel,
        grid=(t // tb,),
        in_specs=[pl.BlockSpec((tb, d), lambda i: (i, 0)),
                  pl.BlockSpec(wq.shape, lambda i: (0, 0)),
                  pl.BlockSpec(keys.shape, lambda i: (0, 0, 0))],
        out_specs=[ospec, ospec],
        out_shape=[out, out],
        compiler_params=_params(("arbitrary",), 48),
        name="route",
    )(h2, wq, keys)


def _scatter_kernel(ee_ref, gt_ref, o_ref, *, tb, grp):
    sub = lax.broadcasted_iota(jnp.int32, (N_KEYS, LANES), 0).astype(F32).astype(BF16)
    one = jnp.ones((), BF16)
    zero = jnp.zeros((), BF16)

    def body(g, carry):
        t0 = pl.multiple_of(g * grp, grp)
        prods = []
        for u in range(grp):
            erow = ee_ref[pl.ds(t0 + u, 1), :]
            ihi = jnp.floor(erow * (1.0 / N_KEYS))
            irow = ihi.astype(BF16)
            jrow = (erow - ihi * float(N_KEYS)).astype(BF16)
            grow = gt_ref[pl.ds(t0 + u, 1), :].astype(BF16)
            r = jnp.where(sub == irow, grow, zero)
            cm = jnp.where(sub == jrow, one, zero)
            prods.append(lax.dot_general(r, cm, _NT, preferred_element_type=F32))
        y = jnp.swapaxes(jnp.stack(prods, axis=0), 0, 1).astype(o_ref.dtype)
        for i in range(N_KEYS):
            o_ref[pl.ds(t0, grp), i * N_KEYS:(i + 1) * N_KEYS] = y[i]
        return carry

    lax.fori_loop(0, tb // grp, body, 0)


def _scatter(ee, gt, *, tb=256, grp=16):
    t, nk = ee.shape
    tb = min(tb, t)
    spec = pl.BlockSpec((tb, nk), lambda i: (i, 0))
    kernel = functools.partial(_scatter_kernel, tb=tb, grp=grp)
    return pl.pallas_call(
        kernel,
        grid=(t // tb,),
        in_specs=[spec, spec],
        out_specs=pl.BlockSpec((tb, N_KEYS * N_KEYS), lambda i: (i, 0)),
        out_shape=jax.ShapeDtypeStruct((t, N_KEYS * N_KEYS), BF16),
        compiler_params=_params(("arbitrary",), 40),
        name="scatter",
    )(ee, gt)


def _peer_kernel(h_ref, ut_ref, v_ref, gm_ref, x1_ref, mod_ref, g_ref, be_ref, o_ref, *, alpha):
    j = pl.program_id(1)

    @pl.when(j == 0)
    def _():
        o_ref[...] = jnp.zeros_like(o_ref)

    act = jnp.dot(h_ref[...], ut_ref[...], preferred_element_type=F32)
    gelu = 0.5 * act * (1.0 + lax.erf(act * (1.0 / math.sqrt(2.0))))
    w = (gm_ref[...].astype(F32) * gelu).astype(BF16)
    o_ref[...] += jnp.dot(w, v_ref[...], preferred_element_type=F32)

    @pl.when(j == pl.num_programs(1) - 1)
    def _():
        gate2 = mod_ref[5:6, :]
        o_ref[...] = _layer_norm(alpha * x1_ref[...] + (1.0 + gate2) * o_ref[...],
                                 g_ref[...], be_ref[...])


def _peer(h2, ut, v, gm, x1, mod, ln_g, ln_b, *, seq, alpha, tb=512, eb=1024):
    t, d = h2.shape
    ne = v.shape[0]
    tb = min(tb, seq)
    per_b = seq // tb
    row = lambda i, j: (i, 0)
    const = lambda i, j: (0, 0)
    kernel = functools.partial(_peer_kernel, alpha=alpha)
    return pl.pallas_call(
        kernel,
        grid=(t // tb, ne // eb),
        in_specs=[pl.BlockSpec((tb, d), row),
                  pl.BlockSpec((d, eb), lambda i, j: (0, j)),
                  pl.BlockSpec((eb, d), lambda i, j: (j, 0)),
                  pl.BlockSpec((tb, eb), lambda i, j: (i, j)),
                  pl.BlockSpec((tb, d), row),
                  pl.BlockSpec((None, 6, d), lambda i, j: (i // per_b, 0, 0)),
                  pl.BlockSpec((1, d), const), pl.BlockSpec((1, d), const)],
        out_specs=pl.BlockSpec((tb, d), row),
        out_shape=jax.ShapeDtypeStruct((t, d), F32),
        compiler_params=_params(("arbitrary", "arbitrary"), 56),
        name="peer",
    )(h2, ut, v, gm, x1, mod, ln_g, ln_b)


def kernel(x, c, w_ada, b_ada, w_in, hg_lb_logits, hg_norm_g, w_branch_a, w_branch_b, w_out,
           ln1_g, ln1_b, peer_wq, peer_subkeys, peer_u, peer_v, ln2_g, ln2_b):
    batch, seq, d = x.shape
    depth = w_ada.shape[0]
    alpha = (2.0 * depth) ** 0.25
    lower_bounds = jnp.cumsum(jax.nn.softmax(hg_lb_logits.astype(F32), axis=0), axis=0)

    r = lax.broadcasted_iota(jnp.int32, (LANES, LANES), 0)
    cc = lax.broadcasted_iota(jnp.int32, (LANES, LANES), 1)
    rb = lax.broadcasted_iota(jnp.int32, (SB_KBLOCK, SB_KBLOCK), 0)
    cb = lax.broadcasted_iota(jnp.int32, (SB_KBLOCK, SB_KBLOCK), 1)
    tri = jnp.concatenate([(rb > cb), jnp.ones((SB_KBLOCK, LANES), bool)], axis=1).astype(BF16)
    ltri = (r[:HG_CHUNK, :HG_CHUNK] >= cc[:HG_CHUNK, :HG_CHUNK]).astype(BF16)
    ones = jnp.ones((LANES, LANES), BF16)

    c_pad = jnp.zeros((8, d), F32).at[:batch].set(c)
    xt = x.reshape(batch * seq, d)
    for layer in range(depth):
        mod = _ada(c_pad, w_ada[layer:layer + 1], b_ada[layer:layer + 1])[:batch].reshape(batch, 6, d)
        proj = _inproj(xt, mod, w_in[layer].astype(BF16), seq=seq)
        a_out = _sbattn(proj, tri, batch=batch, seq=seq)
        b_out = _hgrn(proj, lower_bounds[layer].reshape(HG_HEADS, 1, LANES),
                      hg_norm_g[layer].reshape(HG_HEADS, 1, LANES), ltri, ones, batch=batch, seq=seq)
        x1, h2 = _mix(a_out, b_out, proj, xt, mod, w_branch_a[layer].astype(BF16),
                      w_branch_b[layer].astype(BF16), w_out[layer].astype(BF16),
                      ln1_g[layer:layer + 1], ln1_b[layer:layer + 1], seq=seq, alpha=alpha)
        keys = peer_subkeys[layer].reshape(PEER_HEADS * 2, N_KEYS, LANES).astype(BF16)
        ee, gt = _route(h2, peer_wq[layer].astype(BF16), keys)
        gm = _scatter(ee, gt)
        xt = _peer(h2, peer_u[layer].T.astype(BF16), peer_v[layer].astype(BF16), gm, x1, mod,
                   ln2_g[layer:layer + 1], ln2_b[layer:layer + 1], seq=seq, alpha=alpha)
    return xt.reshape(batch, seq, d)
```

```python
import functools
import math

import jax
import jax.numpy as jnp
from jax import lax
from jax.experimental import pallas as pl
from jax.experimental.pallas import tpu as pltpu

F32 = jnp.float32
BF16 = jnp.bfloat16

LANES = 128
SB_HEADS = 8
SB_QBLOCK = 512
SB_KBLOCK = 256
HG_HEADS = 8
HG_CHUNK = 64
HG_SUB = 16
LOG2E = 1.4426950408889634
PEER_HEADS = 8
PEER_TOPK = 16
N_KEYS = 128
RMS_EPS = 1e-6
LN_EPS = 1e-5
MIB = 1024 * 1024

_NT = (((1,), (1,)), ((), ()))
_TN = (((0,), (0,)), ((), ()))


def _params(semantics, vmem_mib):
    return pltpu.CompilerParams(dimension_semantics=semantics, vmem_limit_bytes=vmem_mib * MIB)


def _ada_kernel(c_ref, w_ref, b_ref, o_ref):
    c = c_ref[...]
    cond = c * jax.nn.sigmoid(c)
    o_ref[...] = jnp.dot(cond.astype(BF16), w_ref[...].astype(BF16),
                         preferred_element_type=F32) + b_ref[...]


def _ada(c_pad, w_ada, b_ada, *, tn=1024):
    rows, d = c_pad.shape
    n = w_ada.shape[-1]
    return pl.pallas_call(
        _ada_kernel,
        grid=(n // tn,),
        in_specs=[pl.BlockSpec((rows, d), lambda j: (0, 0)),
                  pl.BlockSpec((None, d, tn), lambda j: (0, 0, j)),
                  pl.BlockSpec((1, tn), lambda j: (0, j))],
        out_specs=pl.BlockSpec((rows, tn), lambda j: (0, j)),
        out_shape=jax.ShapeDtypeStruct((rows, n), F32),
        compiler_params=_params(("arbitrary",), 40),
        name="ada",
    )(c_pad, w_ada, b_ada)


def _inproj_kernel(x_ref, mod_ref, w_ref, o_ref, h_scr):
    @pl.when(pl.program_id(1) == 0)
    def _():
        shift = mod_ref[0:1, :]
        scale = mod_ref[1:2, :]
        h_scr[...] = (x_ref[...] * (1.0 + scale) + shift).astype(BF16)

    o_ref[...] = jnp.dot(h_scr[...], w_ref[...], preferred_element_type=F32).astype(BF16)


def _inproj(x2, mod, w_in, *, seq, tm=1024, tn=1024):
    t, d = x2.shape
    n = w_in.shape[1]
    tm = min(tm, seq)
    per_b = seq // tm
    return pl.pallas_call(
        _inproj_kernel,
        grid=(t // tm, n // tn),
        in_specs=[pl.BlockSpec((tm, d), lambda i, j: (i, 0)),
                  pl.BlockSpec((None, 6, d), lambda i, j: (i // per_b, 0, 0)),
                  pl.BlockSpec((d, tn), lambda i, j: (0, j))],
        out_specs=pl.BlockSpec((tm, tn), lambda i, j: (i, j)),
        out_shape=jax.ShapeDtypeStruct((t, n), BF16),
        scratch_shapes=[pltpu.VMEM((tm, d), BF16)],
        compiler_params=_params(("arbitrary", "arbitrary"), 48),
        name="inproj",
    )(x2, mod, w_in)


def _sb_kernel(q_ref, k_ref, v_ref, tri_ref, o_ref, *, tq, tk, scale):
    qi = pl.program_id(1)
    nd = tq // tk
    q = (q_ref[...].astype(F32) * scale).astype(BF16)
    tri = tri_ref[...]

    def tile(j, acc, out, r0=None):
        ks = pl.multiple_of(j * tk, tk)
        z = lax.dot_general(q[r0:], k_ref[pl.ds(ks, tk), :], _NT, preferred_element_type=F32)
        lp = jnp.log(1.0 + jnp.exp2(jnp.abs(z) * (-LOG2E)))
        logb = jnp.minimum(z, 0.0) - lp
        l1mb = logb - z
        if r0 is not None:
            mask = (lax.broadcasted_iota(jnp.int32, z.shape, 1)
                    < lax.broadcasted_iota(jnp.int32, z.shape, 0))
            l1mb = jnp.where(mask, l1mb, 0.0)
            logb = jnp.where(mask, logb, -1e30)
        cs = jnp.dot(l1mb.astype(BF16), tri, preferred_element_type=F32)
        w = jnp.exp(logb + cs[:, :tk] + jnp.concatenate([acc[r0:]] * (tk // LANES), axis=1))
        pv = jnp.dot(w.astype(BF16), v_ref[pl.ds(ks, tk), :], preferred_element_type=F32)
        if r0:
            pad = jnp.zeros((r0, LANES), F32)
            return acc + jnp.concatenate([pad, cs[:, tk:]], axis=0), out + jnp.concatenate([pad, pv], axis=0)
        return acc + cs[:, tk:], out + pv

    acc = jnp.zeros((tq, LANES), F32)
    out = jnp.zeros((tq, LANES), F32)
    for dd in reversed(range(nd)):
        acc, out = tile(qi * nd + dd, acc, out, dd * tk)

    def steps(j, n, carry):
        a, o = carry
        for u in range(n):
            a, o = tile(j - u, a, o, None)
        return a, o

    pairs = qi // 2
    acc, out = lax.fori_loop(
        0, pairs, lambda i, c: steps((qi - 2 * i) * nd - 1, 2 * nd, c), (acc, out))
    acc, out = lax.fori_loop(
        0, qi - 2 * pairs, lambda i, c: steps(nd - 1, nd, c), (acc, out))
    o_ref[...] = out.astype(o_ref.dtype)


def _sbattn(proj, tri, *, batch, seq, tq=SB_QBLOCK, tk=SB_KBLOCK):
    t = proj.shape[0]
    tq = min(tq, seq)
    nq = seq // tq
    kernel = functools.partial(_sb_kernel, tq=tq, tk=tk, scale=1.0 / math.sqrt(LANES))
    return pl.pallas_call(
        kernel,
        grid=(batch * SB_HEADS, nq),
        in_specs=[pl.BlockSpec((tq, LANES), lambda bh, qi: ((bh // SB_HEADS) * nq + qi, bh % SB_HEADS)),
                  pl.BlockSpec((seq, LANES), lambda bh, qi: (bh // SB_HEADS, SB_HEADS + bh % SB_HEADS)),
                  pl.BlockSpec((seq, LANES), lambda bh, qi: (bh // SB_HEADS, 2 * SB_HEADS + bh % SB_HEADS)),
                  pl.BlockSpec((tk, tk + LANES), lambda bh, qi: (0, 0))],
        out_specs=pl.BlockSpec((tq, LANES), lambda bh, qi: ((bh // SB_HEADS) * nq + qi, bh % SB_HEADS)),
        out_shape=jax.ShapeDtypeStruct((t, SB_HEADS * LANES), BF16),
        compiler_params=_params(("arbitrary", "arbitrary"), 40),
        name="sbattn",
    )(proj, proj, proj, tri)


def _hgrn_kernel(q_ref, f_ref, i_ref, g_ref, lb_ref, ng_ref, ltri_ref, ones_ref, o_ref, st_ref,
                 *, rb, c, sb):
    @pl.when(pl.program_id(1) == 0)
    def _():
        st_ref[...] = jnp.zeros_like(st_ref)

    lb = lb_ref[...]
    ng = ng_ref[...]
    ltri = ltri_ref[...]
    ones = ones_ref[...]
    lane8 = lax.broadcasted_iota(jnp.int32, (8, c), 1)
    sub8 = lax.broadcasted_iota(jnp.int32, (8, LANES), 0)

    pre = []
    for ci in range(rb // c):
        r0 = ci * c
        qf = q_ref[r0:r0 + c, :].astype(F32)
        ff = f_ref[r0:r0 + c, :].astype(F32)
        vb = i_ref[r0:r0 + c, :]
        fg = lb + (1.0 - lb) * jax.nn.sigmoid(ff)
        lf = jnp.log(fg)
        kk = 1.0 - fg
        hi = lf.astype(BF16)
        lo = (lf - hi.astype(F32)).astype(BF16)
        b = (jnp.dot(ltri, hi, preferred_element_type=F32)
             + jnp.dot(ltri, lo, preferred_element_type=F32))

        b2 = b * LOG2E

        groups = [jnp.zeros((8, c), F32), jnp.zeros((8, c), F32)]
        for i in range(1, c // sb):
            r = b2[i * sb - 1:i * sb, :]
            qs = (qf[i * sb:(i + 1) * sb, :] * jnp.exp2(b2[i * sb:(i + 1) * sb, :] - r)).astype(BF16)
            ks = (kk[:i * sb, :] * jnp.exp2(r - b2[:i * sb, :])).astype(BF16)
            ks = jnp.concatenate([ks, jnp.zeros((c - i * sb, LANES), BF16)], axis=0)
            off_diag = lax.dot_general(qs, ks, _NT, preferred_element_type=F32)
            groups.extend([off_diag[:8, :], off_diag[8:, :]])

        slabs = []
        for s in range(c):
            g0, end = (s // 8) * 8, (s // sb + 1) * sb
            p = qf[g0:end, :] * kk[s:s + 1, :] * jnp.exp2(b2[g0:end, :] - b2[s:s + 1, :])
            if s % 8:
                head = jnp.where(sub8 >= (s % 8), p[:8, :], 0.0)
                p = head if end - g0 == 8 else jnp.concatenate([head, p[8:, :]], axis=0)
            slabs.append(p)
        red = jnp.dot(jnp.concatenate(slabs, axis=0).astype(BF16), ones, preferred_element_type=F32)
        off = 0
        for s in range(c):
            for g in range(s // 8, (s // sb + 1) * (sb // 8)):
                groups[g] = jnp.where(lane8 == s, red[off:off + 8, :c], groups[g])
                off += 8
        scores = jnp.concatenate(groups, axis=0)

        o_intra = jnp.dot(scores.astype(BF16), vb, preferred_element_type=F32)
        b_last = b2[c - 1:c, :]
        qt = (qf * jnp.exp2(b2)).astype(BF16)
        kt = (kk * jnp.exp2(b_last - b2)).astype(BF16)
        upd = lax.dot_general(vb, kt, _TN, preferred_element_type=F32)
        pre.append((o_intra, qt, upd, jnp.exp2(b_last)))

    st = st_ref[...]
    for ci, (o_intra, qt, upd, dec) in enumerate(pre):
        r0 = ci * c
        o = o_intra + lax.dot_general(qt, st.astype(BF16), _NT, preferred_element_type=F32)
        st = st * dec + upd
        o = o * lax.rsqrt(jnp.mean(o * o, axis=-1, keepdims=True) + RMS_EPS)
        gg = g_ref[r0:r0 + c, :].astype(F32)
        o_ref[r0:r0 + c, :] = (o * ng * (gg * jax.nn.sigmoid(gg))).astype(o_ref.dtype)
    st_ref[...] = st


def _hgrn(proj, lb, ng, ltri, ones, *, batch, seq, rb=1024):
    t = proj.shape[0]
    rb = min(rb, seq)
    nr = seq // rb
    base = 3 * SB_HEADS

    def col(k):
        return pl.BlockSpec((rb, LANES),
                            lambda bh, r: ((bh // HG_HEADS) * nr + r, base + k * HG_HEADS + bh % HG_HEADS))

    vec = pl.BlockSpec((None, 1, LANES), lambda bh, r: (bh % HG_HEADS, 0, 0))
    kernel = functools.partial(_hgrn_kernel, rb=rb, c=HG_CHUNK, sb=HG_SUB)
    return pl.pallas_call(
        kernel,
        grid=(batch * HG_HEADS, nr),
        in_specs=[col(0), col(1), col(2), col(3), vec, vec,
                  pl.BlockSpec((HG_CHUNK, HG_CHUNK), lambda bh, r: (0, 0)),
                  pl.BlockSpec((LANES, LANES), lambda bh, r: (0, 0))],
        out_specs=pl.BlockSpec((rb, LANES), lambda bh, r: ((bh // HG_HEADS) * nr + r, bh % HG_HEADS)),
        out_shape=jax.ShapeDtypeStruct((t, HG_HEADS * LANES), BF16),
        scratch_shapes=[pltpu.VMEM((LANES, LANES), F32)],
        compiler_params=_params(("arbitrary", "arbitrary"), 32),
        name="hgrn",
    )(proj, proj, proj, proj, lb, ng, ltri, ones)


def _layer_norm(z, g, b):
    mu = jnp.mean(z, axis=-1, keepdims=True)
    zc = z - mu
    var = jnp.mean(zc * zc, axis=-1, keepdims=True)
    return zc * lax.rsqrt(var + LN_EPS) * g + b


def _mix_kernel(a_ref, b_ref, ga0_ref, ga1_ref, gb0_ref, gb1_ref, x_ref, mod_ref, wa_ref, wb_ref,
                wo_ref, g_ref, be_ref, x1_ref, h2_ref, *, alpha):
    a = a_ref[...]
    b = b_ref[...]
    half = wa_ref.shape[1] // 2
    merged = []
    for n, (ga_ref, gb_ref) in enumerate(((ga0_ref, gb0_ref), (ga1_ref, gb1_ref))):
        pa = jnp.dot(a, wa_ref[:, n * half:(n + 1) * half], preferred_element_type=F32)
        pb = jnp.dot(b, wb_ref[:, n * half:(n + 1) * half], preferred_element_type=F32)
        m = (jax.nn.sigmoid(ga_ref[...].astype(F32)) * pa
             + jax.nn.sigmoid(gb_ref[...].astype(F32)) * pb)
        merged.append(m.astype(BF16))
    y = jnp.dot(jnp.concatenate(merged, axis=1), wo_ref[...], preferred_element_type=F32)
    gate1 = mod_ref[2:3, :]
    shift2 = mod_ref[3:4, :]
    scale2 = mod_ref[4:5, :]
    x1 = _layer_norm(alpha * x_ref[...] + (1.0 + gate1) * y, g_ref[...], be_ref[...])
    x1_ref[...] = x1
    h2_ref[...] = (x1 * (1.0 + scale2) + shift2).astype(BF16)


def _mix(a_out, b_out, proj, x2, mod, wa, wb, wo, ln_g, ln_b, *, seq, alpha, tm=512):
    t, d = x2.shape
    w = a_out.shape[1]
    tm = min(tm, seq)
    per_b = seq // tm
    gbase = (3 * SB_HEADS + 4 * HG_HEADS) * LANES // w
    row = lambda i: (i, 0)
    const = lambda i: (0, 0)

    def gate(k):
        return pl.BlockSpec((tm, w), lambda i: (i, gbase + k))

    kernel = functools.partial(_mix_kernel, alpha=alpha)
    return pl.pallas_call(
        kernel,
        grid=(t // tm,),
        in_specs=[pl.BlockSpec((tm, w), row), pl.BlockSpec((tm, w), row),
                  gate(0), gate(1), gate(2), gate(3),
                  pl.BlockSpec((tm, d), row),
                  pl.BlockSpec((None, 6, d), lambda i: (i // per_b, 0, 0)),
                  pl.BlockSpec((w, d), const, pipeline_mode=pl.Buffered(1)),
                  pl.BlockSpec((w, d), const, pipeline_mode=pl.Buffered(1)),
                  pl.BlockSpec((d, d), const, pipeline_mode=pl.Buffered(1)),
                  pl.BlockSpec((1, d), const), pl.BlockSpec((1, d), const)],
        out_specs=[pl.BlockSpec((tm, d), row), pl.BlockSpec((tm, d), row)],
        out_shape=[jax.ShapeDtypeStruct((t, d), F32), jax.ShapeDtypeStruct((t, d), BF16)],
        compiler_params=_params(("arbitrary",), 56),
        name="mix",
    )(a_out, b_out, proj, proj, proj, proj, x2, mod, wa, wb, wo, ln_g, ln_b)


def _top_rows(vals, payloads, k, n_rows):
    rid = lax.broadcasted_iota(jnp.int32, vals.shape, 0).astype(F32)
    best, rows, outs = [], [], [[] for _ in payloads]
    for _ in range(k):
        m = jnp.max(vals, axis=0, keepdims=True)
        first = jnp.min(jnp.where(vals == m, rid, float(n_rows)), axis=0, keepdims=True)
        sel = rid == first
        best.append(m)
        rows.append(first)
        for o, p in zip(outs, payloads):
            o.append(jnp.sum(jnp.where(sel, p, 0.0), axis=0, keepdims=True))
        vals = jnp.where(sel, -jnp.inf, vals)
    return best, rows, outs


def _route_kernel(h_ref, wq_ref, keys_ref, ee_ref, gt_ref):
    tb = h_ref.shape[0]
    k = PEER_TOPK
    q = jnp.dot(h_ref[...], wq_ref[...], preferred_element_type=F32).astype(BF16)
    sub8 = lax.broadcasted_iota(jnp.int32, (8, tb), 0)
    ee, gt = [], []
    for h in range(PEER_HEADS):
        tops, idxs = [], []
        for p in range(2):
            c0 = (h * 2 + p) * LANES
            s = lax.dot_general(keys_ref[h * 2 + p], q[:, c0:c0 + LANES], _NT,
                                preferred_element_type=F32)
            best, rows, _ = _top_rows(s, [], k, N_KEYS)
            tops.append(best)
            idxs.append(rows)
        top1 = jnp.concatenate(tops[1], axis=0)
        idx1 = jnp.concatenate(idxs[1], axis=0)
        cand = [tops[0][0] + top1]
        ce = [idxs[0][0] * float(N_KEYS) + idx1]
        for a in range(1, 8):
            ca = tops[0][a] + top1[:8]
            if k // (a + 1) < 8:
                ca = jnp.where(sub8 < k // (a + 1), ca, -jnp.inf)
            cand.append(ca)
            ce.append(idxs[0][a] * float(N_KEYS) + idx1[:8])
        cand.append(jnp.concatenate(tops[0][8:], axis=0) + top1[:1])
        ce.append(jnp.concatenate(idxs[0][8:], axis=0) * float(N_KEYS) + idx1[:1])
        cand = jnp.concatenate(cand, axis=0)
        best, _, (be,) = _top_rows(cand, [jnp.concatenate(ce, axis=0)], k, cand.shape[0])
        e = [jnp.exp(v - best[0]) for v in best]
        denom = functools.reduce(lambda a, b: a + b, e)
        inv = 1.0 / denom
        ee.extend(be)
        gt.extend([v * inv for v in e])
    ee_ref[...] = jnp.concatenate(ee, axis=0).T
    gt_ref[...] = jnp.concatenate(gt, axis=0).T


def _route(h2, wq, keys, *, tb=256):
    t, d = h2.shape
    tb = min(tb, t)
    nk = PEER_HEADS * PEER_TOPK
    out = jax.ShapeDtypeStruct((t, nk), F32)
    ospec = pl.BlockSpec((tb, nk), lambda i: (i, 0))
    return pl.pallas_call(
        _route_kernel,
        grid=(t // tb,),
        in_specs=[pl.BlockSpec((tb, d), lambda i: (i, 0)),
                  pl.BlockSpec(wq.shape, lambda i: (0, 0)),
                  pl.BlockSpec(keys.shape, lambda i: (0, 0, 0))],
        out_specs=[ospec, ospec],
        out_shape=[out, out],
        compiler_params=_params(("arbitrary",), 48),
        name="route",
    )(h2, wq, keys)


def _scatter_kernel(ee_ref, gt_ref, o_ref, *, tb, grp):
    sub = lax.broadcasted_iota(jnp.int32, (N_KEYS, LANES), 0).astype(F32).astype(BF16)
    one = jnp.ones((), BF16)
    zero = jnp.zeros((), BF16)

    def body(g, carry):
        t0 = pl.multiple_of(g * grp, grp)
        prods = []
        for u in range(grp):
            erow = ee_ref[pl.ds(t0 + u, 1), :]
            ihi = jnp.floor(erow * (1.0 / N_KEYS))
            irow = ihi.astype(BF16)
            jrow = (erow - ihi * float(N_KEYS)).astype(BF16)
            grow = gt_ref[pl.ds(t0 + u, 1), :].astype(BF16)
            r = jnp.where(sub == irow, grow, zero)
            cm = jnp.where(sub == jrow, one, zero)
            prods.append(lax.dot_general(r, cm, _NT, preferred_element_type=F32))
        y = jnp.swapaxes(jnp.stack(prods, axis=0), 0, 1).astype(o_ref.dtype)
        for i in range(N_KEYS):
            o_ref[pl.ds(t0, grp), i * N_KEYS:(i + 1) * N_KEYS] = y[i]
        return carry

    lax.fori_loop(0, tb // grp, body, 0)


def _scatter(ee, gt, *, tb=256, grp=16):
    t, nk = ee.shape
    tb = min(tb, t)
    spec = pl.BlockSpec((tb, nk), lambda i: (i, 0))
    kernel = functools.partial(_scatter_kernel, tb=tb, grp=grp)
    return pl.pallas_call(
        kernel,
        grid=(t // tb,),
        in_specs=[spec, spec],
        out_specs=pl.BlockSpec((tb, N_KEYS * N_KEYS), lambda i: (i, 0)),
        out_shape=jax.ShapeDtypeStruct((t, N_KEYS * N_KEYS), BF16),
        compiler_params=_params(("arbitrary",), 40),
        name="scatter",
    )(ee, gt)


def _peer_kernel(h_ref, ut_ref, v_ref, gm_ref, x1_ref, mod_ref, g_ref, be_ref, o_ref, *, alpha):
    j = pl.program_id(1)

    @pl.when(j == 0)
    def _():
        o_ref[...] = jnp.zeros_like(o_ref)

    act = jnp.dot(h_ref[...], ut_ref[...], preferred_element_type=F32)
    gelu = 0.5 * act * (1.0 + lax.erf(act * (1.0 / math.sqrt(2.0))))
    w = (gm_ref[...].astype(F32) * gelu).astype(BF16)
    o_ref[...] += jnp.dot(w, v_ref[...], preferred_element_type=F32)

    @pl.when(j == pl.num_programs(1) - 1)
    def _():
        gate2 = mod_ref[5:6, :]
        o_ref[...] = _layer_norm(alpha * x1_ref[...] + (1.0 + gate2) * o_ref[...],
                                 g_ref[...], be_ref[...])


def _peer(h2, ut, v, gm, x1, mod, ln_g, ln_b, *, seq, alpha, tb=1024, eb=512, vmem_mib=62):
    t, d = h2.shape
    ne = v.shape[0]
    tb = min(tb, seq)
    per_b = seq // tb
    row = lambda i, j: (i, 0)
    const = lambda i, j: (0, 0)
    kernel = functools.partial(_peer_kernel, alpha=alpha)
    return pl.pallas_call(
        kernel,
        grid=(t // tb, ne // eb),
        in_specs=[pl.BlockSpec((tb, d), row),
                  pl.BlockSpec((d, eb), lambda i, j: (0, j)),
                  pl.BlockSpec((eb, d), lambda i, j: (j, 0)),
                  pl.BlockSpec((tb, eb), lambda i, j: (i, j)),
                  pl.BlockSpec((tb, d), row),
                  pl.BlockSpec((None, 6, d), lambda i, j: (i // per_b, 0, 0)),
                  pl.BlockSpec((1, d), const), pl.BlockSpec((1, d), const)],
        out_specs=pl.BlockSpec((tb, d), row),
        out_shape=jax.ShapeDtypeStruct((t, d), F32),
        compiler_params=_params(("arbitrary", "arbitrary"), vmem_mib),
        name="peer",
    )(h2, ut, v, gm, x1, mod, ln_g, ln_b)


def kernel(x, c, w_ada, b_ada, w_in, hg_lb_logits, hg_norm_g, w_branch_a, w_branch_b, w_out,
           ln1_g, ln1_b, peer_wq, peer_subkeys, peer_u, peer_v, ln2_g, ln2_b):
    batch, seq, d = x.shape
    depth = w_ada.shape[0]
    alpha = (2.0 * depth) ** 0.25
    lower_bounds = jnp.cumsum(jax.nn.softmax(hg_lb_logits.astype(F32), axis=0), axis=0)

    r = lax.broadcasted_iota(jnp.int32, (LANES, LANES), 0)
    cc = lax.broadcasted_iota(jnp.int32, (LANES, LANES), 1)
    rb = lax.broadcasted_iota(jnp.int32, (SB_KBLOCK, SB_KBLOCK), 0)
    cb = lax.broadcasted_iota(jnp.int32, (SB_KBLOCK, SB_KBLOCK), 1)
    tri = jnp.concatenate([(rb > cb), jnp.ones((SB_KBLOCK, LANES), bool)], axis=1).astype(BF16)
    ltri = (r[:HG_CHUNK, :HG_CHUNK] >= cc[:HG_CHUNK, :HG_CHUNK]).astype(BF16)
    ones = jnp.ones((LANES, LANES), BF16)

    c_pad = jnp.zeros((8, d), F32).at[:batch].set(c)
    xt = x.reshape(batch * seq, d)
    for layer in range(depth):
        mod = _ada(c_pad, w_ada[layer:layer + 1], b_ada[layer:layer + 1])[:batch].reshape(batch, 6, d)
        proj = _inproj(xt, mod, w_in[layer].astype(BF16), seq=seq)
        a_out = _sbattn(proj, tri, batch=batch, seq=seq)
        b_out = _hgrn(proj, lower_bounds[layer].reshape(HG_HEADS, 1, LANES),
                      hg_norm_g[layer].reshape(HG_HEADS, 1, LANES), ltri, ones, batch=batch, seq=seq)
        x1, h2 = _mix(a_out, b_out, proj, xt, mod, w_branch_a[layer].astype(BF16),
                      w_branch_b[layer].astype(BF16), w_out[layer].astype(BF16),
                      ln1_g[layer:layer + 1], ln1_b[layer:layer + 1], seq=seq, alpha=alpha)
        keys = peer_subkeys[layer].reshape(PEER_HEADS * 2, N_KEYS, LANES).astype(BF16)
        ee, gt = _route(h2, peer_wq[layer].astype(BF16), keys)
        gm = _scatter(ee, gt)
        xt = _peer(h2, peer_u[layer].T.astype(BF16), peer_v[layer].astype(BF16), gm, x1, mod,
                   ln2_g[layer:layer + 1], ln2_b[layer:layer + 1], seq=seq, alpha=alpha)
    return xt.reshape(batch, seq, d)
```

```python
import functools
import math

import jax
import jax.numpy as jnp
from jax import lax
from jax.experimental import pallas as pl
from jax.experimental.pallas import tpu as pltpu

F32 = jnp.float32
BF16 = jnp.bfloat16

LANES = 128
SB_HEADS = 8
SB_QBLOCK = 512
SB_KBLOCK = 256
HG_HEADS = 8
HG_CHUNK = 64
HG_SUB = 16
LOG2E = 1.4426950408889634
PEER_HEADS = 8
PEER_TOPK = 16
N_KEYS = 128
RMS_EPS = 1e-6
LN_EPS = 1e-5
MIB = 1024 * 1024

_NT = (((1,), (1,)), ((), ()))
_TN = (((0,), (0,)), ((), ()))


def _params(semantics, vmem_mib):
    return pltpu.CompilerParams(dimension_semantics=semantics, vmem_limit_bytes=vmem_mib * MIB)


def _ada_kernel(c_ref, w_ref, b_ref, o_ref):
    c = c_ref[...]
    cond = c * jax.nn.sigmoid(c)
    o_ref[...] = jnp.dot(cond.astype(BF16), w_ref[...].astype(BF16),
                         preferred_element_type=F32) + b_ref[...]


def _ada(c_pad, w_ada, b_ada, *, tn=1024):
    rows, d = c_pad.shape
    n = w_ada.shape[-1]
    return pl.pallas_call(
        _ada_kernel,
        grid=(n // tn,),
        in_specs=[pl.BlockSpec((rows, d), lambda j: (0, 0)),
                  pl.BlockSpec((None, d, tn), lambda j: (0, 0, j)),
                  pl.BlockSpec((1, tn), lambda j: (0, j))],
        out_specs=pl.BlockSpec((rows, tn), lambda j: (0, j)),
        out_shape=jax.ShapeDtypeStruct((rows, n), F32),
        compiler_params=_params(("arbitrary",), 40),
        name="ada",
    )(c_pad, w_ada, b_ada)


def _inproj_kernel(x_ref, mod_ref, w_ref, o_ref, h_scr):
    @pl.when(pl.program_id(1) == 0)
    def _():
        shift = mod_ref[0:1, :]
        scale = mod_ref[1:2, :]
        h_scr[...] = (x_ref[...] * (1.0 + scale) + shift).astype(BF16)

    o_ref[...] = jnp.dot(h_scr[...], w_ref[...], preferred_element_type=F32).astype(BF16)


def _inproj(x2, mod, w_in, *, seq, tm=1024, tn=1024):
    t, d = x2.shape
    n = w_in.shape[1]
    tm = min(tm, seq)
    per_b = seq // tm
    return pl.pallas_call(
        _inproj_kernel,
        grid=(t // tm, n // tn),
        in_specs=[pl.BlockSpec((tm, d), lambda i, j: (i, 0)),
                  pl.BlockSpec((None, 6, d), lambda i, j: (i // per_b, 0, 0)),
                  pl.BlockSpec((d, tn), lambda i, j: (0, j))],
        out_specs=pl.BlockSpec((tm, tn), lambda i, j: (i, j)),
        out_shape=jax.ShapeDtypeStruct((t, n), BF16),
        scratch_shapes=[pltpu.VMEM((tm, d), BF16)],
        compiler_params=_params(("arbitrary", "arbitrary"), 48),
        name="inproj",
    )(x2, mod, w_in)


def _sb_kernel(q_ref, k_ref, v_ref, tri_ref, o_ref, *, tq, tk, scale):
    qi = pl.program_id(1)
    nd = tq // tk
    q = (q_ref[...].astype(F32) * scale).astype(BF16)
    tri = tri_ref[...]

    def tile(j, acc, out, r0=None):
        ks = pl.multiple_of(j * tk, tk)
        z = lax.dot_general(q[r0:], k_ref[pl.ds(ks, tk), :], _NT, preferred_element_type=F32)
        lp = jnp.log(1.0 + jnp.exp2(jnp.abs(z) * (-LOG2E)))
        logb = jnp.minimum(z, 0.0) - lp
        l1mb = logb - z
        if r0 is not None:
            mask = (lax.broadcasted_iota(jnp.int32, z.shape, 1)
                    < lax.broadcasted_iota(jnp.int32, z.shape, 0))
            l1mb = jnp.where(mask, l1mb, 0.0)
            logb = jnp.where(mask, logb, -1e30)
        cs = jnp.dot(l1mb.astype(BF16), tri, preferred_element_type=F32)
        w = jnp.exp(logb + cs[:, :tk] + jnp.concatenate([acc[r0:]] * (tk // LANES), axis=1))
        pv = jnp.dot(w.astype(BF16), v_ref[pl.ds(ks, tk), :], preferred_element_type=F32)
        if r0:
            pad = jnp.zeros((r0, LANES), F32)
            return acc + jnp.concatenate([pad, cs[:, tk:]], axis=0), out + jnp.concatenate([pad, pv], axis=0)
        return acc + cs[:, tk:], out + pv

    acc = jnp.zeros((tq, LANES), F32)
    out = jnp.zeros((tq, LANES), F32)
    for dd in reversed(range(nd)):
        acc, out = tile(qi * nd + dd, acc, out, dd * tk)

    def steps(j, n, carry):
        a, o = carry
        for u in range(n):
            a, o = tile(j - u, a, o, None)
        return a, o

    pairs = qi // 2
    acc, out = lax.fori_loop(
        0, pairs, lambda i, c: steps((qi - 2 * i) * nd - 1, 2 * nd, c), (acc, out))
    acc, out = lax.fori_loop(
        0, qi - 2 * pairs, lambda i, c: steps(nd - 1, nd, c), (acc, out))
    o_ref[...] = out.astype(o_ref.dtype)


def _sbattn(proj, tri, *, batch, seq, tq=SB_QBLOCK, tk=SB_KBLOCK):
    t = proj.shape[0]
    tq = min(tq, seq)
    nq = seq // tq
    kernel = functools.partial(_sb_kernel, tq=tq, tk=tk, scale=1.0 / math.sqrt(LANES))
    return pl.pallas_call(
        kernel,
        grid=(batch * SB_HEADS, nq),
        in_specs=[pl.BlockSpec((tq, LANES), lambda bh, qi: ((bh // SB_HEADS) * nq + qi, bh % SB_HEADS)),
                  pl.BlockSpec((seq, LANES), lambda bh, qi: (bh // SB_HEADS, SB_HEADS + bh % SB_HEADS)),
                  pl.BlockSpec((seq, LANES), lambda bh, qi: (bh // SB_HEADS, 2 * SB_HEADS + bh % SB_HEADS)),
                  pl.BlockSpec((tk, tk + LANES), lambda bh, qi: (0, 0))],
        out_specs=pl.BlockSpec((tq, LANES), lambda bh, qi: ((bh // SB_HEADS) * nq + qi, bh % SB_HEADS)),
        out_shape=jax.ShapeDtypeStruct((t, SB_HEADS * LANES), BF16),
        compiler_params=_params(("arbitrary", "arbitrary"), 40),
        name="sbattn",
    )(proj, proj, proj, tri)


def _hgrn_kernel(q_ref, f_ref, i_ref, g_ref, lb_ref, ng_ref, ltri_ref, ones_ref, o_ref, st_ref,
                 *, rb, c, sb):
    @pl.when(pl.program_id(1) == 0)
    def _():
        st_ref[...] = jnp.zeros_like(st_ref)

    lb = lb_ref[...]
    ng = ng_ref[...]
    ltri = ltri_ref[...]
    ones = ones_ref[...]
    lane8 = lax.broadcasted_iota(jnp.int32, (8, c), 1)
    sub8 = lax.broadcasted_iota(jnp.int32, (8, LANES), 0)

    pre = []
    for ci in range(rb // c):
        r0 = ci * c
        qf = q_ref[r0:r0 + c, :].astype(F32)
        ff = f_ref[r0:r0 + c, :].astype(F32)
        vb = i_ref[r0:r0 + c, :]
        fg = lb + (1.0 - lb) * jax.nn.sigmoid(ff)
        lf = jnp.log(fg)
        kk = 1.0 - fg
        hi = lf.astype(BF16)
        lo = (lf - hi.astype(F32)).astype(BF16)
        b = (jnp.dot(ltri, hi, preferred_element_type=F32)
             + jnp.dot(ltri, lo, preferred_element_type=F32))

        b2 = b * LOG2E

        groups = [jnp.zeros((8, c), F32), jnp.zeros((8, c), F32)]
        for i in range(1, c // sb):
            r = b2[i * sb - 1:i * sb, :]
            qs = (qf[i * sb:(i + 1) * sb, :] * jnp.exp2(b2[i * sb:(i + 1) * sb, :] - r)).astype(BF16)
            ks = (kk[:i * sb, :] * jnp.exp2(r - b2[:i * sb, :])).astype(BF16)
            ks = jnp.concatenate([ks, jnp.zeros((c - i * sb, LANES), BF16)], axis=0)
            off_diag = lax.dot_general(qs, ks, _NT, preferred_element_type=F32)
            groups.extend([off_diag[:8, :], off_diag[8:, :]])

        slabs = []
        for s in range(c):
            g0, end = (s // 8) * 8, (s // sb + 1) * sb
            p = qf[g0:end, :] * kk[s:s + 1, :] * jnp.exp2(b2[g0:end, :] - b2[s:s + 1, :])
            if s % 8:
                head = jnp.where(sub8 >= (s % 8), p[:8, :], 0.0)
                p = head if end - g0 == 8 else jnp.concatenate([head, p[8:, :]], axis=0)
            slabs.append(p)
        red = jnp.dot(jnp.concatenate(slabs, axis=0).astype(BF16), ones, preferred_element_type=F32)
        off = 0
        for s in range(c):
            for g in range(s // 8, (s // sb + 1) * (sb // 8)):
                groups[g] = jnp.where(lane8 == s, red[off:off + 8, :c], groups[g])
                off += 8
        scores = jnp.concatenate(groups, axis=0)

        o_intra = jnp.dot(scores.astype(BF16), vb, preferred_element_type=F32)
        b_last = b2[c - 1:c, :]
        qt = (qf * jnp.exp2(b2)).astype(BF16)
        kt = (kk * jnp.exp2(b_last - b2)).astype(BF16)
        upd = lax.dot_general(vb, kt, _TN, preferred_element_type=F32)
        pre.append((o_intra, qt, upd, jnp.exp2(b_last)))

    st = st_ref[...]
    for ci, (o_intra, qt, upd, dec) in enumerate(pre):
        r0 = ci * c
        o = o_intra + lax.dot_general(qt, st.astype(BF16), _NT, preferred_element_type=F32)
        st = st * dec + upd
        o = o * lax.rsqrt(jnp.mean(o * o, axis=-1, keepdims=True) + RMS_EPS)
        gg = g_ref[r0:r0 + c, :].astype(F32)
        o_ref[r0:r0 + c, :] = (o * ng * (gg * jax.nn.sigmoid(gg))).astype(o_ref.dtype)
    st_ref[...] = st


def _hgrn(proj, lb, ng, ltri, ones, *, batch, seq, rb=2048):
    t = proj.shape[0]
    rb = min(rb, seq)
    nr = seq // rb
    base = 3 * SB_HEADS

    def col(k):
        return pl.BlockSpec((rb, LANES),
                            lambda bh, r: ((bh // HG_HEADS) * nr + r, base + k * HG_HEADS + bh % HG_HEADS))

    vec = pl.BlockSpec((None, 1, LANES), lambda bh, r: (bh % HG_HEADS, 0, 0))
    kernel = functools.partial(_hgrn_kernel, rb=rb, c=HG_CHUNK, sb=HG_SUB)
    return pl.pallas_call(
        kernel,
        grid=(batch * HG_HEADS, nr),
        in_specs=[col(0), col(1), col(2), col(3), vec, vec,
                  pl.BlockSpec((HG_CHUNK, HG_CHUNK), lambda bh, r: (0, 0)),
                  pl.BlockSpec((LANES, LANES), lambda bh, r: (0, 0))],
        out_specs=pl.BlockSpec((rb, LANES), lambda bh, r: ((bh // HG_HEADS) * nr + r, bh % HG_HEADS)),
        out_shape=jax.ShapeDtypeStruct((t, HG_HEADS * LANES), BF16),
        scratch_shapes=[pltpu.VMEM((LANES, LANES), F32)],
        compiler_params=_params(("arbitrary", "arbitrary"), 32),
        name="hgrn",
    )(proj, proj, proj, proj, lb, ng, ltri, ones)


def _layer_norm(z, g, b):
    mu = jnp.mean(z, axis=-1, keepdims=True)
    zc = z - mu
    var = jnp.mean(zc * zc, axis=-1, keepdims=True)
    return zc * lax.rsqrt(var + LN_EPS) * g + b


def _mix_kernel(a_ref, b_ref, ga0_ref, ga1_ref, gb0_ref, gb1_ref, x_ref, mod_ref, wa_ref, wb_ref,
                wo_ref, g_ref, be_ref, x1_ref, h2_ref, *, alpha):
    a = a_ref[...]
    b = b_ref[...]
    half = wa_ref.shape[1] // 2
    merged = []
    for n, (ga_ref, gb_ref) in enumerate(((ga0_ref, gb0_ref), (ga1_ref, gb1_ref))):
        pa = jnp.dot(a, wa_ref[:, n * half:(n + 1) * half], preferred_element_type=F32)
        pb = jnp.dot(b, wb_ref[:, n * half:(n + 1) * half], preferred_element_type=F32)
        m = (jax.nn.sigmoid(ga_ref[...].astype(F32)) * pa
             + jax.nn.sigmoid(gb_ref[...].astype(F32)) * pb)
        merged.append(m.astype(BF16))
    y = jnp.dot(jnp.concatenate(merged, axis=1), wo_ref[...], preferred_element_type=F32)
    gate1 = mod_ref[2:3, :]
    shift2 = mod_ref[3:4, :]
    scale2 = mod_ref[4:5, :]
    x1 = _layer_norm(alpha * x_ref[...] + (1.0 + gate1) * y, g_ref[...], be_ref[...])
    x1_ref[...] = x1
    h2_ref[...] = (x1 * (1.0 + scale2) + shift2).astype(BF16)


def _mix(a_out, b_out, proj, x2, mod, wa, wb, wo, ln_g, ln_b, *, seq, alpha, tm=512):
    t, d = x2.shape
    w = a_out.shape[1]
    tm = min(tm, seq)
    per_b = seq // tm
    gbase = (3 * SB_HEADS + 4 * HG_HEADS) * LANES // w
    row = lambda i: (i, 0)
    const = lambda i: (0, 0)

    def gate(k):
        return pl.BlockSpec((tm, w), lambda i: (i, gbase + k))

    kernel = functools.partial(_mix_kernel, alpha=alpha)
    return pl.pallas_call(
        kernel,
        grid=(t // tm,),
        in_specs=[pl.BlockSpec((tm, w), row), pl.BlockSpec((tm, w), row),
                  gate(0), gate(1), gate(2), gate(3),
                  pl.BlockSpec((tm, d), row),
                  pl.BlockSpec((None, 6, d), lambda i: (i // per_b, 0, 0)),
                  pl.BlockSpec((w, d), const, pipeline_mode=pl.Buffered(1)),
                  pl.BlockSpec((w, d), const, pipeline_mode=pl.Buffered(1)),
                  pl.BlockSpec((d, d), const, pipeline_mode=pl.Buffered(1)),
                  pl.BlockSpec((1, d), const), pl.BlockSpec((1, d), const)],
        out_specs=[pl.BlockSpec((tm, d), row), pl.BlockSpec((tm, d), row)],
        out_shape=[jax.ShapeDtypeStruct((t, d), F32), jax.ShapeDtypeStruct((t, d), BF16)],
        compiler_params=_params(("arbitrary",), 56),
        name="mix",
    )(a_out, b_out, proj, proj, proj, proj, x2, mod, wa, wb, wo, ln_g, ln_b)


def _top_rows(vals, payloads, k, n_rows):
    rid = lax.broadcasted_iota(jnp.int32, vals.shape, 0).astype(F32)
    best, rows, outs = [], [], [[] for _ in payloads]
    for _ in range(k):
        m = jnp.max(vals, axis=0, keepdims=True)
        first = jnp.min(jnp.where(vals == m, rid, float(n_rows)), axis=0, keepdims=True)
        sel = rid == first
        best.append(m)
        rows.append(first)
        for o, p in zip(outs, payloads):
            o.append(jnp.sum(jnp.where(sel, p, 0.0), axis=0, keepdims=True))
        vals = jnp.where(sel, -jnp.inf, vals)
    return best, rows, outs


def _route_kernel(h_ref, wq_ref, keys_ref, ee_ref, gt_ref):
    tb = h_ref.shape[0]
    k = PEER_TOPK
    q = jnp.dot(h_ref[...], wq_ref[...], preferred_element_type=F32).astype(BF16)
    ee, gt = [], []
    for h in range(PEER_HEADS):
        tops, idxs = [], []
        for p in range(2):
            c0 = (h * 2 + p) * LANES
            s = lax.dot_general(keys_ref[h * 2 + p], q[:, c0:c0 + LANES], _NT,
                                preferred_element_type=F32)
            best, rows, _ = _top_rows(s, [], k, N_KEYS)
            tops.append(best)
            idxs.append(rows)
        top1 = jnp.concatenate(tops[1], axis=0)
        idx1 = jnp.concatenate(idxs[1], axis=0)
        cand, ce = [], []
        for a in range(8):
            nb = k // (a + 1)
            cand.append(tops[0][a] + top1[:nb])
            ce.append(idxs[0][a] * float(N_KEYS) + idx1[:nb])
        npad = -sum(c.shape[0] for c in cand) % 8
        cand.append(jnp.full((npad, tb), -jnp.inf, F32))
        ce.append(jnp.zeros((npad, tb), F32))
        cand.append(jnp.concatenate(tops[0][8:], axis=0) + top1[:1])
        ce.append(jnp.concatenate(idxs[0][8:], axis=0) * float(N_KEYS) + idx1[:1])
        cand = jnp.concatenate(cand, axis=0)
        best, _, (be,) = _top_rows(cand, [jnp.concatenate(ce, axis=0)], k, cand.shape[0])
        e = [jnp.exp(v - best[0]) for v in best]
        denom = functools.reduce(lambda a, b: a + b, e)
        inv = 1.0 / denom
        ee.extend(be)
        gt.extend([v * inv for v in e])
    ee_ref[...] = jnp.concatenate(ee, axis=0).T
    gt_ref[...] = jnp.concatenate(gt, axis=0).T


def _route(h2, wq, keys, *, tb=256):
    t, d = h2.shape
    tb = min(tb, t)
    nk = PEER_HEADS * PEER_TOPK
    out = jax.ShapeDtypeStruct((t, nk), F32)
    ospec = pl.BlockSpec((tb, nk), lambda i: (i, 0))
    return pl.pallas_call(
        _route_kernel,
        grid=(t // tb,),
        in_specs=[pl.BlockSpec((tb, d), lambda i: (i, 0)),
                  pl.BlockSpec(wq.shape, lambda i: (0, 0)),
                  pl.BlockSpec(keys.shape, lambda i: (0, 0, 0))],
        out_specs=[ospec, ospec],
        out_shape=[out, out],
        compiler_params=_params(("arbitrary",), 48),
        name="route",
    )(h2, wq, keys)


def _scatter_kernel(ee_ref, gt_ref, o_ref, *, tb, grp):
    sub = lax.broadcasted_iota(jnp.int32, (N_KEYS, LANES), 0).astype(F32).astype(BF16)
    one = jnp.ones((), BF16)
    zero = jnp.zeros((), BF16)

    def body(g, carry):
        t0 = pl.multiple_of(g * grp, grp)
        prods = []
        for u in range(grp):
            erow = ee_ref[pl.ds(t0 + u, 1), :]
            ihi = jnp.floor(erow * (1.0 / N_KEYS))
            irow = ihi.astype(BF16)
            jrow = (erow - ihi * float(N_KEYS)).astype(BF16)
            grow = gt_ref[pl.ds(t0 + u, 1), :].astype(BF16)
            r = jnp.where(sub == irow, grow, zero)
            cm = jnp.where(sub == jrow, one, zero)
            prods.append(lax.dot_general(r, cm, _NT, preferred_element_type=F32))
        y = jnp.swapaxes(jnp.stack(prods, axis=0), 0, 1).astype(o_ref.dtype)
        for i in range(N_KEYS):
            o_ref[pl.ds(t0, grp), i * N_KEYS:(i + 1) * N_KEYS] = y[i]
        return carry

    lax.fori_loop(0, tb // grp, body, 0)


def _scatter(ee, gt, *, tb=256, grp=16):
    t, nk = ee.shape
    tb = min(tb, t)
    spec = pl.BlockSpec((tb, nk), lambda i: (i, 0))
    kernel = functools.partial(_scatter_kernel, tb=tb, grp=grp)
    return pl.pallas_call(
        kernel,
        grid=(t // tb,),
        in_specs=[spec, spec],
        out_specs=pl.BlockSpec((tb, N_KEYS * N_KEYS), lambda i: (i, 0)),
        out_shape=jax.ShapeDtypeStruct((t, N_KEYS * N_KEYS), BF16),
        compiler_params=_params(("arbitrary",), 40),
        name="scatter",
    )(ee, gt)


def _peer_kernel(h_ref, u_ref, v_ref, gm_ref, x1_ref, mod_ref, g_ref, be_ref, o_ref, *, alpha):
    j = pl.program_id(1)

    @pl.when(j == 0)
    def _():
        o_ref[...] = jnp.zeros_like(o_ref)

    act = lax.dot_general(h_ref[...], u_ref[...], _NT, preferred_element_type=F32)
    gelu = 0.5 * act * (1.0 + lax.erf(act * (1.0 / math.sqrt(2.0))))
    w = (gm_ref[...].astype(F32) * gelu).astype(BF16)
    o_ref[...] += jnp.dot(w, v_ref[...], preferred_element_type=F32)

    @pl.when(j == pl.num_programs(1) - 1)
    def _():
        gate2 = mod_ref[5:6, :]
        o_ref[...] = _layer_norm(alpha * x1_ref[...] + (1.0 + gate2) * o_ref[...],
                                 g_ref[...], be_ref[...])


def _peer(h2, u, v, gm, x1, mod, ln_g, ln_b, *, seq, alpha, tb=1024, eb=512, vmem_mib=62):
    t, d = h2.shape
    ne = v.shape[0]
    tb = min(tb, seq)
    per_b = seq // tb
    row = lambda i, j: (i, 0)
    const = lambda i, j: (0, 0)
    kernel = functools.partial(_peer_kernel, alpha=alpha)
    return pl.pallas_call(
        kernel,
        grid=(t // tb, ne // eb),
        in_specs=[pl.BlockSpec((tb, d), row),
                  pl.BlockSpec((eb, d), lambda i, j: (j, 0)),
                  pl.BlockSpec((eb, d), lambda i, j: (j, 0)),
                  pl.BlockSpec((tb, eb), lambda i, j: (i, j)),
                  pl.BlockSpec((tb, d), row),
                  pl.BlockSpec((None, 6, d), lambda i, j: (i // per_b, 0, 0)),
                  pl.BlockSpec((1, d), const), pl.BlockSpec((1, d), const)],
        out_specs=pl.BlockSpec((tb, d), row),
        out_shape=jax.ShapeDtypeStruct((t, d), F32),
        compiler_params=_params(("arbitrary", "arbitrary"), vmem_mib),
        name="peer",
    )(h2, u, v, gm, x1, mod, ln_g, ln_b)


def kernel(x, c, w_ada, b_ada, w_in, hg_lb_logits, hg_norm_g, w_branch_a, w_branch_b, w_out,
           ln1_g, ln1_b, peer_wq, peer_subkeys, peer_u, peer_v, ln2_g, ln2_b):
    batch, seq, d = x.shape
    depth = w_ada.shape[0]
    alpha = (2.0 * depth) ** 0.25
    lower_bounds = jnp.cumsum(jax.nn.softmax(hg_lb_logits.astype(F32), axis=0), axis=0)

    r = lax.broadcasted_iota(jnp.int32, (LANES, LANES), 0)
    cc = lax.broadcasted_iota(jnp.int32, (LANES, LANES), 1)
    rb = lax.broadcasted_iota(jnp.int32, (SB_KBLOCK, SB_KBLOCK), 0)
    cb = lax.broadcasted_iota(jnp.int32, (SB_KBLOCK, SB_KBLOCK), 1)
    tri = jnp.concatenate([(rb > cb), jnp.ones((SB_KBLOCK, LANES), bool)], axis=1).astype(BF16)
    ltri = (r[:HG_CHUNK, :HG_CHUNK] >= cc[:HG_CHUNK, :HG_CHUNK]).astype(BF16)
    ones = jnp.ones((LANES, LANES), BF16)

    c_pad = jnp.zeros((8, d), F32).at[:batch].set(c)
    xt = x.reshape(batch * seq, d)
    for layer in range(depth):
        mod = _ada(c_pad, w_ada[layer:layer + 1], b_ada[layer:layer + 1])[:batch].reshape(batch, 6, d)
        proj = _inproj(xt, mod, w_in[layer].astype(BF16), seq=seq)
        a_out = _sbattn(proj, tri, batch=batch, seq=seq)
        b_out = _hgrn(proj, lower_bounds[layer].reshape(HG_HEADS, 1, LANES),
                      hg_norm_g[layer].reshape(HG_HEADS, 1, LANES), ltri, ones, batch=batch, seq=seq)
        x1, h2 = _mix(a_out, b_out, proj, xt, mod, w_branch_a[layer].astype(BF16),
                      w_branch_b[layer].astype(BF16), w_out[layer].astype(BF16),
                      ln1_g[layer:layer + 1], ln1_b[layer:layer + 1], seq=seq, alpha=alpha)
        keys = peer_subkeys[layer].reshape(PEER_HEADS * 2, N_KEYS, LANES).astype(BF16)
        ee, gt = _route(h2, peer_wq[layer].astype(BF16), keys)
        gm = _scatter(ee, gt)
        xt = _peer(h2, peer_u[layer].astype(BF16), peer_v[layer].astype(BF16), gm, x1, mod,
                   ln2_g[layer:layer + 1], ln2_b[layer:layer + 1], seq=seq, alpha=alpha)
    return xt.reshape(batch, seq, d)
```

```python
import functools
import math

import jax
import jax.numpy as jnp
from jax import lax
from jax.experimental import pallas as pl
from jax.experimental.pallas import tpu as pltpu

F32 = jnp.float32
BF16 = jnp.bfloat16

LANES = 128
SB_HEADS = 8
SB_QBLOCK = 512
SB_KBLOCK = 256
HG_HEADS = 8
HG_CHUNK = 64
HG_SUB = 16
LOG2E = 1.4426950408889634
PEER_HEADS = 8
PEER_TOPK = 16
N_KEYS = 128
RMS_EPS = 1e-6
LN_EPS = 1e-5
MIB = 1024 * 1024

_NT = (((1,), (1,)), ((), ()))
_TN = (((0,), (0,)), ((), ()))


def _params(semantics, vmem_mib):
    return pltpu.CompilerParams(dimension_semantics=semantics, vmem_limit_bytes=vmem_mib * MIB)


def _ada_kernel(c_ref, w_ref, b_ref, o_ref):
    c = c_ref[...]
    cond = c * jax.nn.sigmoid(c)
    o_ref[...] = jnp.dot(cond.astype(BF16), w_ref[...].astype(BF16),
                         preferred_element_type=F32) + b_ref[...]


def _ada(c_pad, w_ada, b_ada, *, tn=1024):
    rows, d = c_pad.shape
    n = w_ada.shape[-1]
    return pl.pallas_call(
        _ada_kernel,
        grid=(n // tn,),
        in_specs=[pl.BlockSpec((rows, d), lambda j: (0, 0)),
                  pl.BlockSpec((None, d, tn), lambda j: (0, 0, j)),
                  pl.BlockSpec((1, tn), lambda j: (0, j))],
        out_specs=pl.BlockSpec((rows, tn), lambda j: (0, j)),
        out_shape=jax.ShapeDtypeStruct((rows, n), F32),
        compiler_params=_params(("arbitrary",), 40),
        name="ada",
    )(c_pad, w_ada, b_ada)


def _inproj_kernel(x_ref, mod_ref, w_ref, o_ref, h_scr):
    @pl.when(pl.program_id(1) == 0)
    def _():
        shift = mod_ref[0:1, :]
        scale = mod_ref[1:2, :]
        h_scr[...] = (x_ref[...] * (1.0 + scale) + shift).astype(BF16)

    o_ref[...] = jnp.dot(h_scr[...], w_ref[...], preferred_element_type=F32).astype(BF16)


def _inproj(x2, mod, w_in, *, seq, tm=1024, tn=1024):
    t, d = x2.shape
    n = w_in.shape[1]
    tm = min(tm, seq)
    per_b = seq // tm
    return pl.pallas_call(
        _inproj_kernel,
        grid=(t // tm, n // tn),
        in_specs=[pl.BlockSpec((tm, d), lambda i, j: (i, 0)),
                  pl.BlockSpec((None, 6, d), lambda i, j: (i // per_b, 0, 0)),
                  pl.BlockSpec((d, tn), lambda i, j: (0, j))],
        out_specs=pl.BlockSpec((tm, tn), lambda i, j: (i, j)),
        out_shape=jax.ShapeDtypeStruct((t, n), BF16),
        scratch_shapes=[pltpu.VMEM((tm, d), BF16)],
        compiler_params=_params(("arbitrary", "arbitrary"), 48),
        name="inproj",
    )(x2, mod, w_in)


def _sb_kernel(q_ref, k_ref, v_ref, tri_ref, o_ref, *, tq, tk, scale):
    qi = pl.program_id(1)
    nd = tq // tk
    q = (q_ref[...].astype(F32) * scale).astype(BF16)
    tri = tri_ref[...]

    def tile(j, acc, out, r0=None):
        ks = pl.multiple_of(j * tk, tk)
        z = lax.dot_general(q[r0:], k_ref[pl.ds(ks, tk), :], _NT, preferred_element_type=F32)
        lp = jnp.log(1.0 + jnp.exp2(jnp.abs(z) * (-LOG2E)))
        logb = jnp.minimum(z, 0.0) - lp
        l1mb = logb - z
        if r0 is not None:
            mask = (lax.broadcasted_iota(jnp.int32, z.shape, 1)
                    < lax.broadcasted_iota(jnp.int32, z.shape, 0))
            l1mb = jnp.where(mask, l1mb, 0.0)
            logb = jnp.where(mask, logb, -1e30)
        cs = jnp.dot(l1mb.astype(BF16), tri, preferred_element_type=F32)
        w = jnp.exp(logb + cs[:, :tk] + jnp.concatenate([acc[r0:]] * (tk // LANES), axis=1))
        pv = jnp.dot(w.astype(BF16), v_ref[pl.ds(ks, tk), :], preferred_element_type=F32)
        if r0:
            pad = jnp.zeros((r0, LANES), F32)
            return acc + jnp.concatenate([pad, cs[:, tk:]], axis=0), out + jnp.concatenate([pad, pv], axis=0)
        return acc + cs[:, tk:], out + pv

    acc = jnp.zeros((tq, LANES), F32)
    out = jnp.zeros((tq, LANES), F32)
    for dd in reversed(range(nd)):
        acc, out = tile(qi * nd + dd, acc, out, dd * tk)

    def steps(j, n, carry):
        a, o = carry
        for u in range(n):
            a, o = tile(j - u, a, o, None)
        return a, o

    pairs = qi // 2
    acc, out = lax.fori_loop(
        0, pairs, lambda i, c: steps((qi - 2 * i) * nd - 1, 2 * nd, c), (acc, out))
    acc, out = lax.fori_loop(
        0, qi - 2 * pairs, lambda i, c: steps(nd - 1, nd, c), (acc, out))
    o_ref[...] = out.astype(o_ref.dtype)


def _sbattn(proj, tri, *, batch, seq, tq=SB_QBLOCK, tk=SB_KBLOCK):
    t = proj.shape[0]
    tq = min(tq, seq)
    nq = seq // tq
    kernel = functools.partial(_sb_kernel, tq=tq, tk=tk, scale=1.0 / math.sqrt(LANES))
    return pl.pallas_call(
        kernel,
        grid=(batch * SB_HEADS, nq),
        in_specs=[pl.BlockSpec((tq, LANES), lambda bh, qi: ((bh // SB_HEADS) * nq + qi, bh % SB_HEADS)),
                  pl.BlockSpec((seq, LANES), lambda bh, qi: (bh // SB_HEADS, SB_HEADS + bh % SB_HEADS)),
                  pl.BlockSpec((seq, LANES), lambda bh, qi: (bh // SB_HEADS, 2 * SB_HEADS + bh % SB_HEADS)),
                  pl.BlockSpec((tk, tk + LANES), lambda bh, qi: (0, 0))],
        out_specs=pl.BlockSpec((tq, LANES), lambda bh, qi: ((bh // SB_HEADS) * nq + qi, bh % SB_HEADS)),
        out_shape=jax.ShapeDtypeStruct((t, SB_HEADS * LANES), BF16),
        compiler_params=_params(("arbitrary", "arbitrary"), 40),
        name="sbattn",
    )(proj, proj, proj, tri)


def _hgrn_kernel(q_ref, f_ref, i_ref, g_ref, lb_ref, ng_ref, ltri_ref, ones_ref, wa_ref, wb_ref,
                 o_ref, wa16_ref, wb16_ref, st_ref, *, rb, c, sb):
    wa16_ref[...] = wa_ref[...].astype(BF16)
    wb16_ref[...] = wb_ref[...].astype(BF16)

    @pl.when(pl.program_id(1) == 0)
    def _():
        st_ref[...] = jnp.zeros_like(st_ref)

    lb = lb_ref[...]
    ng = ng_ref[...]
    ltri = ltri_ref[...]
    ones = ones_ref[...]
    lane8 = lax.broadcasted_iota(jnp.int32, (8, c), 1)
    sub8 = lax.broadcasted_iota(jnp.int32, (8, LANES), 0)

    pre = []
    for ci in range(rb // c):
        r0 = ci * c
        qf = q_ref[r0:r0 + c, :].astype(F32)
        ff = f_ref[r0:r0 + c, :].astype(F32)
        vb = i_ref[r0:r0 + c, :]
        fg = lb + (1.0 - lb) * jax.nn.sigmoid(ff)
        lf = jnp.log(fg)
        kk = 1.0 - fg
        hi = lf.astype(BF16)
        lo = (lf - hi.astype(F32)).astype(BF16)
        b = (jnp.dot(ltri, hi, preferred_element_type=F32)
             + jnp.dot(ltri, lo, preferred_element_type=F32))

        b2 = b * LOG2E

        groups = [jnp.zeros((8, c), F32), jnp.zeros((8, c), F32)]
        for i in range(1, c // sb):
            r = b2[i * sb - 1:i * sb, :]
            qs = (qf[i * sb:(i + 1) * sb, :] * jnp.exp2(b2[i * sb:(i + 1) * sb, :] - r)).astype(BF16)
            ks = (kk[:i * sb, :] * jnp.exp2(r - b2[:i * sb, :])).astype(BF16)
            ks = jnp.concatenate([ks, jnp.zeros((c - i * sb, LANES), BF16)], axis=0)
            off_diag = lax.dot_general(qs, ks, _NT, preferred_element_type=F32)
            groups.extend([off_diag[:8, :], off_diag[8:, :]])

        slabs = []
        for s in range(c):
            g0, end = (s // 8) * 8, (s // sb + 1) * sb
            p = qf[g0:end, :] * kk[s:s + 1, :] * jnp.exp2(b2[g0:end, :] - b2[s:s + 1, :])
            if s % 8:
                head = jnp.where(sub8 >= (s % 8), p[:8, :], 0.0)
                p = head if end - g0 == 8 else jnp.concatenate([head, p[8:, :]], axis=0)
            slabs.append(p)
        red = jnp.dot(jnp.concatenate(slabs, axis=0).astype(BF16), ones, preferred_element_type=F32)
        off = 0
        for s in range(c):
            for g in range(s // 8, (s // sb + 1) * (sb // 8)):
                groups[g] = jnp.where(lane8 == s, red[off:off + 8, :c], groups[g])
                off += 8
        scores = jnp.concatenate(groups, axis=0)

        o_intra = jnp.dot(scores.astype(BF16), vb, preferred_element_type=F32)
        b_last = b2[c - 1:c, :]
        qt = (qf * jnp.exp2(b2)).astype(BF16)
        kt = (kk * jnp.exp2(b_last - b2)).astype(BF16)
        upd = lax.dot_general(vb, kt, _TN, preferred_element_type=F32)
        pre.append((o_intra, qt, upd, jnp.exp2(b_last)))

    st = st_ref[...]
    for ci, (o_intra, qt, upd, dec) in enumerate(pre):
        r0 = ci * c
        o = o_intra + lax.dot_general(qt, st.astype(BF16), _NT, preferred_element_type=F32)
        st = st * dec + upd
        o = o * lax.rsqrt(jnp.mean(o * o, axis=-1, keepdims=True) + RMS_EPS)
        gg = g_ref[r0:r0 + c, :].astype(F32)
        o_ref[r0:r0 + c, :] = (o * ng * (gg * jax.nn.sigmoid(gg))).astype(o_ref.dtype)
    st_ref[...] = st


def _hgrn(proj, lb, ng, ltri, ones, wa, wb, *, batch, seq, rb=2048):
    t = proj.shape[0]
    rb = min(rb, seq)
    nr = seq // rb
    nsteps = batch * HG_HEADS * nr
    base = 3 * SB_HEADS

    def col(k):
        return pl.BlockSpec((rb, LANES),
                            lambda bh, r: ((bh // HG_HEADS) * nr + r, base + k * HG_HEADS + bh % HG_HEADS))

    def slab(w):
        return pl.BlockSpec((w.shape[0] // nsteps, w.shape[1]), lambda bh, r: (bh * nr + r, 0))

    vec = pl.BlockSpec((None, 1, LANES), lambda bh, r: (bh % HG_HEADS, 0, 0))
    kernel = functools.partial(_hgrn_kernel, rb=rb, c=HG_CHUNK, sb=HG_SUB)
    return pl.pallas_call(
        kernel,
        grid=(batch * HG_HEADS, nr),
        in_specs=[col(0), col(1), col(2), col(3), vec, vec,
                  pl.BlockSpec((HG_CHUNK, HG_CHUNK), lambda bh, r: (0, 0)),
                  pl.BlockSpec((LANES, LANES), lambda bh, r: (0, 0)),
                  slab(wa), slab(wb)],
        out_specs=[pl.BlockSpec((rb, LANES), lambda bh, r: ((bh // HG_HEADS) * nr + r, bh % HG_HEADS)),
                   slab(wa), slab(wb)],
        out_shape=[jax.ShapeDtypeStruct((t, HG_HEADS * LANES), BF16),
                   jax.ShapeDtypeStruct(wa.shape, BF16), jax.ShapeDtypeStruct(wb.shape, BF16)],
        scratch_shapes=[pltpu.VMEM((LANES, LANES), F32)],
        compiler_params=_params(("arbitrary", "arbitrary"), 32),
        name="hgrn",
    )(proj, proj, proj, proj, lb, ng, ltri, ones, wa, wb)


def _layer_norm(z, g, b):
    mu = jnp.mean(z, axis=-1, keepdims=True)
    zc = z - mu
    var = jnp.mean(zc * zc, axis=-1, keepdims=True)
    return zc * lax.rsqrt(var + LN_EPS) * g + b


def _mix_kernel(a_ref, b_ref, ga0_ref, ga1_ref, gb0_ref, gb1_ref, x_ref, mod_ref, wa_ref, wb_ref,
                wo_ref, g_ref, be_ref, x1_ref, h2_ref, *, alpha):
    a = a_ref[...]
    b = b_ref[...]
    half = wa_ref.shape[1] // 2
    merged = []
    for n, (ga_ref, gb_ref) in enumerate(((ga0_ref, gb0_ref), (ga1_ref, gb1_ref))):
        pa = jnp.dot(a, wa_ref[:, n * half:(n + 1) * half], preferred_element_type=F32)
        pb = jnp.dot(b, wb_ref[:, n * half:(n + 1) * half], preferred_element_type=F32)
        m = (jax.nn.sigmoid(ga_ref[...].astype(F32)) * pa
             + jax.nn.sigmoid(gb_ref[...].astype(F32)) * pb)
        merged.append(m.astype(BF16))
    y = jnp.dot(jnp.concatenate(merged, axis=1), wo_ref[...], preferred_element_type=F32)
    gate1 = mod_ref[2:3, :]
    shift2 = mod_ref[3:4, :]
    scale2 = mod_ref[4:5, :]
    x1 = _layer_norm(alpha * x_ref[...] + (1.0 + gate1) * y, g_ref[...], be_ref[...])
    x1_ref[...] = x1
    h2_ref[...] = (x1 * (1.0 + scale2) + shift2).astype(BF16)


def _mix(a_out, b_out, proj, x2, mod, wa, wb, wo, ln_g, ln_b, *, seq, alpha, tm=512):
    t, d = x2.shape
    w = a_out.shape[1]
    tm = min(tm, seq)
    per_b = seq // tm
    gbase = (3 * SB_HEADS + 4 * HG_HEADS) * LANES // w
    row = lambda i: (i, 0)
    const = lambda i: (0, 0)

    def gate(k):
        return pl.BlockSpec((tm, w), lambda i: (i, gbase + k))

    kernel = functools.partial(_mix_kernel, alpha=alpha)
    return pl.pallas_call(
        kernel,
        grid=(t // tm,),
        in_specs=[pl.BlockSpec((tm, w), row), pl.BlockSpec((tm, w), row),
                  gate(0), gate(1), gate(2), gate(3),
                  pl.BlockSpec((tm, d), row),
                  pl.BlockSpec((None, 6, d), lambda i: (i // per_b, 0, 0)),
                  pl.BlockSpec((w, d), const, pipeline_mode=pl.Buffered(1)),
                  pl.BlockSpec((w, d), const, pipeline_mode=pl.Buffered(1)),
                  pl.BlockSpec((d, d), const, pipeline_mode=pl.Buffered(1)),
                  pl.BlockSpec((1, d), const), pl.BlockSpec((1, d), const)],
        out_specs=[pl.BlockSpec((tm, d), row), pl.BlockSpec((tm, d), row)],
        out_shape=[jax.ShapeDtypeStruct((t, d), F32), jax.ShapeDtypeStruct((t, d), BF16)],
        compiler_params=_params(("arbitrary",), 56),
        name="mix",
    )(a_out, b_out, proj, proj, proj, proj, x2, mod, wa, wb, wo, ln_g, ln_b)


def _top_rows(vals, payloads, k, n_rows):
    rid = lax.broadcasted_iota(jnp.int32, vals.shape, 0).astype(F32)
    best, rows, outs = [], [], [[] for _ in payloads]
    for _ in range(k):
        m = jnp.max(vals, axis=0, keepdims=True)
        first = jnp.min(jnp.where(vals == m, rid, float(n_rows)), axis=0, keepdims=True)
        sel = rid == first
        best.append(m)
        rows.append(first)
        for o, p in zip(outs, payloads):
            o.append(jnp.sum(jnp.where(sel, p, 0.0), axis=0, keepdims=True))
        vals = jnp.where(sel, -jnp.inf, vals)
    return best, rows, outs


def _route_kernel(h_ref, wq_ref, keys_ref, tab_ref, ee_ref, gt_ref, tab16_ref):
    tab16_ref[...] = tab_ref[...].astype(BF16)
    tb = h_ref.shape[0]
    k = PEER_TOPK
    q = jnp.dot(h_ref[...], wq_ref[...], preferred_element_type=F32).astype(BF16)
    ee, gt = [], []
    for h in range(PEER_HEADS):
        tops, idxs = [], []
        for p in range(2):
            c0 = (h * 2 + p) * LANES
            s = lax.dot_general(keys_ref[h * 2 + p], q[:, c0:c0 + LANES], _NT,
                                preferred_element_type=F32)
            best, rows, _ = _top_rows(s, [], k, N_KEYS)
            tops.append(best)
            idxs.append(rows)
        top1 = jnp.concatenate(tops[1], axis=0)
        idx1 = jnp.concatenate(idxs[1], axis=0)
        cand, ce = [], []
        for a in range(8):
            nb = k // (a + 1)
            cand.append(tops[0][a] + top1[:nb])
            ce.append(idxs[0][a] * float(N_KEYS) + idx1[:nb])
        npad = -sum(c.shape[0] for c in cand) % 8
        cand.append(jnp.full((npad, tb), -jnp.inf, F32))
        ce.append(jnp.zeros((npad, tb), F32))
        cand.append(jnp.concatenate(tops[0][8:], axis=0) + top1[:1])
        ce.append(jnp.concatenate(idxs[0][8:], axis=0) * float(N_KEYS) + idx1[:1])
        cand = jnp.concatenate(cand, axis=0)
        best, _, (be,) = _top_rows(cand, [jnp.concatenate(ce, axis=0)], k, cand.shape[0])
        e = [jnp.exp(v - best[0]) for v in best]
        denom = functools.reduce(lambda a, b: a + b, e)
        inv = 1.0 / denom
        ee.extend(be)
        gt.extend([v * inv for v in e])
    ee_ref[...] = jnp.concatenate(ee, axis=0).T
    gt_ref[...] = jnp.concatenate(gt, axis=0).T


def _route(h2, wq, keys, table, *, tb=256):
    t, d = h2.shape
    tb = min(tb, t)
    nk = PEER_HEADS * PEER_TOPK
    rows = table.shape[0] // (t // tb)
    out = jax.ShapeDtypeStruct((t, nk), F32)
    ospec = pl.BlockSpec((tb, nk), lambda i: (i, 0))
    tspec = pl.BlockSpec((rows, table.shape[1]), lambda i: (i, 0))
    return pl.pallas_call(
        _route_kernel,
        grid=(t // tb,),
        in_specs=[pl.BlockSpec((tb, d), lambda i: (i, 0)),
                  pl.BlockSpec(wq.shape, lambda i: (0, 0)),
                  pl.BlockSpec(keys.shape, lambda i: (0, 0, 0)),
                  tspec],
        out_specs=[ospec, ospec, tspec],
        out_shape=[out, out, jax.ShapeDtypeStruct(table.shape, BF16)],
        compiler_params=_params(("arbitrary",), 48),
        name="route",
    )(h2, wq, keys, table)


def _scatter_kernel(ee_ref, gt_ref, tab_ref, o_ref, tab16_ref, *, tb, grp):
    tab16_ref[...] = tab_ref[...].astype(BF16)
    sub = lax.broadcasted_iota(jnp.int32, (N_KEYS, LANES), 0).astype(F32).astype(BF16)
    one = jnp.ones((), BF16)
    zero = jnp.zeros((), BF16)

    def body(g, carry):
        t0 = pl.multiple_of(g * grp, grp)
        prods = []
        for u in range(grp):
            erow = ee_ref[pl.ds(t0 + u, 1), :]
            ihi = jnp.floor(erow * (1.0 / N_KEYS))
            irow = ihi.astype(BF16)
            jrow = (erow - ihi * float(N_KEYS)).astype(BF16)
            grow = gt_ref[pl.ds(t0 + u, 1), :].astype(BF16)
            r = jnp.where(sub == irow, grow, zero)
            cm = jnp.where(sub == jrow, one, zero)
            prods.append(lax.dot_general(r, cm, _NT, preferred_element_type=F32))
        y = jnp.swapaxes(jnp.stack(prods, axis=0), 0, 1).astype(o_ref.dtype)
        for i in range(N_KEYS):
            o_ref[pl.ds(t0, grp), i * N_KEYS:(i + 1) * N_KEYS] = y[i]
        return carry

    lax.fori_loop(0, tb // grp, body, 0)


def _scatter(ee, gt, table, *, tb=256, grp=16):
    t, nk = ee.shape
    tb = min(tb, t)
    rows = table.shape[0] // (t // tb)
    spec = pl.BlockSpec((tb, nk), lambda i: (i, 0))
    tspec = pl.BlockSpec((rows, table.shape[1]), lambda i: (i, 0))
    kernel = functools.partial(_scatter_kernel, tb=tb, grp=grp)
    return pl.pallas_call(
        kernel,
        grid=(t // tb,),
        in_specs=[spec, spec, tspec],
        out_specs=[pl.BlockSpec((tb, N_KEYS * N_KEYS), lambda i: (i, 0)), tspec],
        out_shape=[jax.ShapeDtypeStruct((t, N_KEYS * N_KEYS), BF16),
                   jax.ShapeDtypeStruct(table.shape, BF16)],
        compiler_params=_params(("arbitrary",), 48),
        name="scatter",
    )(ee, gt, table)


def _peer_kernel(h_ref, u_ref, v_ref, gm_ref, x1_ref, mod_ref, g_ref, be_ref, o_ref, *, alpha):
    j = pl.program_id(1)

    @pl.when(j == 0)
    def _():
        o_ref[...] = jnp.zeros_like(o_ref)

    act = lax.dot_general(h_ref[...], u_ref[...], _NT, preferred_element_type=F32)
    gelu = 0.5 * act * (1.0 + lax.erf(act * (1.0 / math.sqrt(2.0))))
    w = (gm_ref[...].astype(F32) * gelu).astype(BF16)
    o_ref[...] += jnp.dot(w, v_ref[...], preferred_element_type=F32)

    @pl.when(j == pl.num_programs(1) - 1)
    def _():
        gate2 = mod_ref[5:6, :]
        o_ref[...] = _layer_norm(alpha * x1_ref[...] + (1.0 + gate2) * o_ref[...],
                                 g_ref[...], be_ref[...])


def _peer(h2, u, v, gm, x1, mod, ln_g, ln_b, *, seq, alpha, tb=1024, eb=512, vmem_mib=62):
    t, d = h2.shape
    ne = v.shape[0]
    tb = min(tb, seq)
    per_b = seq // tb
    row = lambda i, j: (i, 0)
    const = lambda i, j: (0, 0)
    kernel = functools.partial(_peer_kernel, alpha=alpha)
    return pl.pallas_call(
        kernel,
        grid=(t // tb, ne // eb),
        in_specs=[pl.BlockSpec((tb, d), row),
                  pl.BlockSpec((eb, d), lambda i, j: (j, 0)),
                  pl.BlockSpec((eb, d), lambda i, j: (j, 0)),
                  pl.BlockSpec((tb, eb), lambda i, j: (i, j)),
                  pl.BlockSpec((tb, d), row),
                  pl.BlockSpec((None, 6, d), lambda i, j: (i // per_b, 0, 0)),
                  pl.BlockSpec((1, d), const), pl.BlockSpec((1, d), const)],
        out_specs=pl.BlockSpec((tb, d), row),
        out_shape=jax.ShapeDtypeStruct((t, d), F32),
        compiler_params=_params(("arbitrary", "arbitrary"), vmem_mib),
        name="peer",
    )(h2, u, v, gm, x1, mod, ln_g, ln_b)


def kernel(x, c, w_ada, b_ada, w_in, hg_lb_logits, hg_norm_g, w_branch_a, w_branch_b, w_out,
           ln1_g, ln1_b, peer_wq, peer_subkeys, peer_u, peer_v, ln2_g, ln2_b):
    batch, seq, d = x.shape
    depth = w_ada.shape[0]
    alpha = (2.0 * depth) ** 0.25
    lower_bounds = jnp.cumsum(jax.nn.softmax(hg_lb_logits.astype(F32), axis=0), axis=0)

    r = lax.broadcasted_iota(jnp.int32, (LANES, LANES), 0)
    cc = lax.broadcasted_iota(jnp.int32, (LANES, LANES), 1)
    rb = lax.broadcasted_iota(jnp.int32, (SB_KBLOCK, SB_KBLOCK), 0)
    cb = lax.broadcasted_iota(jnp.int32, (SB_KBLOCK, SB_KBLOCK), 1)
    tri = jnp.concatenate([(rb > cb), jnp.ones((SB_KBLOCK, LANES), bool)], axis=1).astype(BF16)
    ltri = (r[:HG_CHUNK, :HG_CHUNK] >= cc[:HG_CHUNK, :HG_CHUNK]).astype(BF16)
    ones = jnp.ones((LANES, LANES), BF16)

    c_pad = jnp.zeros((8, d), F32).at[:batch].set(c)
    xt = x.reshape(batch * seq, d)
    for layer in range(depth):
        mod = _ada(c_pad, w_ada[layer:layer + 1], b_ada[layer:layer + 1])[:batch].reshape(batch, 6, d)
        proj = _inproj(xt, mod, w_in[layer].astype(BF16), seq=seq)
        a_out = _sbattn(proj, tri, batch=batch, seq=seq)
        b_out, wo16, wq16 = _hgrn(proj, lower_bounds[layer].reshape(HG_HEADS, 1, LANES),
                                  hg_norm_g[layer].reshape(HG_HEADS, 1, LANES), ltri, ones,
                                  w_out[layer], peer_wq[layer], batch=batch, seq=seq)
        x1, h2 = _mix(a_out, b_out, proj, xt, mod, w_branch_a[layer].astype(BF16),
                      w_branch_b[layer].astype(BF16), wo16,
                      ln1_g[layer:layer + 1], ln1_b[layer:layer + 1], seq=seq, alpha=alpha)
        keys = peer_subkeys[layer].reshape(PEER_HEADS * 2, N_KEYS, LANES).astype(BF16)
        ee, gt, u16 = _route(h2, wq16, keys, peer_u[layer])
        gm, v16 = _scatter(ee, gt, peer_v[layer])
        xt = _peer(h2, u16, v16, gm, x1, mod,
                   ln2_g[layer:layer + 1], ln2_b[layer:layer + 1], seq=seq, alpha=alpha)
    return xt.reshape(batch, seq, d)
```

```python
import functools
import math

import jax
import jax.numpy as jnp
from jax import lax
from jax.experimental import pallas as pl
from jax.experimental.pallas import tpu as pltpu

F32 = jnp.float32
BF16 = jnp.bfloat16

LANES = 128
SB_HEADS = 8
SB_QBLOCK = 512
SB_KBLOCK = 256
HG_HEADS = 8
HG_CHUNK = 64
HG_SUB = 16
LOG2E = 1.4426950408889634
PEER_HEADS = 8
PEER_TOPK = 16
N_KEYS = 128
RMS_EPS = 1e-6
LN_EPS = 1e-5
MIB = 1024 * 1024

_NT = (((1,), (1,)), ((), ()))
_TN = (((0,), (0,)), ((), ()))


def _params(semantics, vmem_mib):
    return pltpu.CompilerParams(dimension_semantics=semantics, vmem_limit_bytes=vmem_mib * MIB)


def _ada_kernel(c_ref, w_ref, b_ref, o_ref):
    c = c_ref[...]
    cond = c * jax.nn.sigmoid(c)
    o_ref[...] = jnp.dot(cond.astype(BF16), w_ref[...].astype(BF16),
                         preferred_element_type=F32) + b_ref[...]


def _ada(c_pad, w_ada, b_ada, *, tn=1024):
    rows, d = c_pad.shape
    n = w_ada.shape[-1]
    return pl.pallas_call(
        _ada_kernel,
        grid=(n // tn,),
        in_specs=[pl.BlockSpec((rows, d), lambda j: (0, 0)),
                  pl.BlockSpec((None, d, tn), lambda j: (0, 0, j)),
                  pl.BlockSpec((1, tn), lambda j: (0, j))],
        out_specs=pl.BlockSpec((rows, tn), lambda j: (0, j)),
        out_shape=jax.ShapeDtypeStruct((rows, n), F32),
        compiler_params=_params(("arbitrary",), 40),
        name="ada",
    )(c_pad, w_ada, b_ada)


def _inproj_kernel(x_ref, mod_ref, w_ref, o_ref, h_scr):
    @pl.when(pl.program_id(1) == 0)
    def _():
        shift = mod_ref[0:1, :]
        scale = mod_ref[1:2, :]
        h_scr[...] = (x_ref[...] * (1.0 + scale) + shift).astype(BF16)

    o_ref[...] = jnp.dot(h_scr[...], w_ref[...], preferred_element_type=F32).astype(BF16)


def _inproj(x2, mod, w_in, *, seq, tm=1024, tn=2816, vmem_mib=60):
    t, d = x2.shape
    n = w_in.shape[1]
    tm = min(tm, seq)
    per_b = seq // tm
    return pl.pallas_call(
        _inproj_kernel,
        grid=(t // tm, n // tn),
        in_specs=[pl.BlockSpec((tm, d), lambda i, j: (i, 0)),
                  pl.BlockSpec((None, 6, d), lambda i, j: (i // per_b, 0, 0)),
                  pl.BlockSpec((d, tn), lambda i, j: (0, j))],
        out_specs=pl.BlockSpec((tm, tn), lambda i, j: (i, j)),
        out_shape=jax.ShapeDtypeStruct((t, n), BF16),
        scratch_shapes=[pltpu.VMEM((tm, d), BF16)],
        compiler_params=_params(("arbitrary", "arbitrary"), vmem_mib),
        name="inproj",
    )(x2, mod, w_in)


def _sb_kernel(q_ref, k_ref, v_ref, tri_ref, o_ref, *, tq, tk, scale):
    qi = pl.program_id(1)
    nd = tq // tk
    q = (q_ref[...].astype(F32) * scale).astype(BF16)
    tri = tri_ref[...]

    def tile(j, acc, out, r0=None):
        ks = pl.multiple_of(j * tk, tk)
        z = lax.dot_general(q[r0:], k_ref[pl.ds(ks, tk), :], _NT, preferred_element_type=F32)
        lp = jnp.log(1.0 + jnp.exp2(jnp.abs(z) * (-LOG2E)))
        logb = jnp.minimum(z, 0.0) - lp
        l1mb = logb - z
        if r0 is not None:
            mask = (lax.broadcasted_iota(jnp.int32, z.shape, 1)
                    < lax.broadcasted_iota(jnp.int32, z.shape, 0))
            l1mb = jnp.where(mask, l1mb, 0.0)
            logb = jnp.where(mask, logb, -1e30)
        cs = jnp.dot(l1mb.astype(BF16), tri, preferred_element_type=F32)
        w = jnp.exp(logb + cs[:, :tk] + jnp.concatenate([acc[r0:]] * (tk // LANES), axis=1))
        pv = jnp.dot(w.astype(BF16), v_ref[pl.ds(ks, tk), :], preferred_element_type=F32)
        if r0:
            pad = jnp.zeros((r0, LANES), F32)
            return acc + jnp.concatenate([pad, cs[:, tk:]], axis=0), out + jnp.concatenate([pad, pv], axis=0)
        return acc + cs[:, tk:], out + pv

    acc = jnp.zeros((tq, LANES), F32)
    out = jnp.zeros((tq, LANES), F32)
    for dd in reversed(range(nd)):
        acc, out = tile(qi * nd + dd, acc, out, dd * tk)

    def steps(j, n, carry):
        a, o = carry
        for u in range(n):
            a, o = tile(j - u, a, o, None)
        return a, o

    pairs = qi // 2
    acc, out = lax.fori_loop(
        0, pairs, lambda i, c: steps((qi - 2 * i) * nd - 1, 2 * nd, c), (acc, out))
    acc, out = lax.fori_loop(
        0, qi - 2 * pairs, lambda i, c: steps(nd - 1, nd, c), (acc, out))
    o_ref[...] = out.astype(o_ref.dtype)


def _sbattn(proj, tri, *, batch, seq, tq=SB_QBLOCK, tk=SB_KBLOCK):
    t = proj.shape[0]
    tq = min(tq, seq)
    nq = seq // tq
    kernel = functools.partial(_sb_kernel, tq=tq, tk=tk, scale=1.0 / math.sqrt(LANES))
    return pl.pallas_call(
        kernel,
        grid=(batch * SB_HEADS, nq),
        in_specs=[pl.BlockSpec((tq, LANES), lambda bh, qi: ((bh // SB_HEADS) * nq + qi, bh % SB_HEADS)),
                  pl.BlockSpec((seq, LANES), lambda bh, qi: (bh // SB_HEADS, SB_HEADS + bh % SB_HEADS)),
                  pl.BlockSpec((seq, LANES), lambda bh, qi: (bh // SB_HEADS, 2 * SB_HEADS + bh % SB_HEADS)),
                  pl.BlockSpec((tk, tk + LANES), lambda bh, qi: (0, 0))],
        out_specs=pl.BlockSpec((tq, LANES), lambda bh, qi: ((bh // SB_HEADS) * nq + qi, bh % SB_HEADS)),
        out_shape=jax.ShapeDtypeStruct((t, SB_HEADS * LANES), BF16),
        compiler_params=_params(("arbitrary", "arbitrary"), 40),
        name="sbattn",
    )(proj, proj, proj, tri)


def _hgrn_kernel(q_ref, f_ref, i_ref, g_ref, lb_ref, ng_ref, ltri_ref, ones_ref, *refs, rb, c, sb, nw):
    o_ref, st_ref = refs[nw], refs[2 * nw + 1]
    for w_ref, w16_ref in zip(refs[:nw], refs[nw + 1:2 * nw + 1]):
        w16_ref[...] = w_ref[...].astype(BF16)

    @pl.when(pl.program_id(1) == 0)
    def _():
        st_ref[...] = jnp.zeros_like(st_ref)

    lb = lb_ref[...]
    ng = ng_ref[...]
    ltri = ltri_ref[...]
    ones = ones_ref[...]
    lane8 = lax.broadcasted_iota(jnp.int32, (8, c), 1)
    sub8 = lax.broadcasted_iota(jnp.int32, (8, LANES), 0)

    pre = []
    for ci in range(rb // c):
        r0 = ci * c
        qf = q_ref[r0:r0 + c, :].astype(F32)
        ff = f_ref[r0:r0 + c, :].astype(F32)
        vb = i_ref[r0:r0 + c, :]
        fg = lb + (1.0 - lb) * jax.nn.sigmoid(ff)
        lf = jnp.log(fg)
        kk = 1.0 - fg
        hi = lf.astype(BF16)
        lo = (lf - hi.astype(F32)).astype(BF16)
        b = (jnp.dot(ltri, hi, preferred_element_type=F32)
             + jnp.dot(ltri, lo, preferred_element_type=F32))

        b2 = b * LOG2E

        groups = [jnp.zeros((8, c), F32), jnp.zeros((8, c), F32)]
        for i in range(1, c // sb):
            r = b2[i * sb - 1:i * sb, :]
            qs = (qf[i * sb:(i + 1) * sb, :] * jnp.exp2(b2[i * sb:(i + 1) * sb, :] - r)).astype(BF16)
            ks = (kk[:i * sb, :] * jnp.exp2(r - b2[:i * sb, :])).astype(BF16)
            ks = jnp.concatenate([ks, jnp.zeros((c - i * sb, LANES), BF16)], axis=0)
            off_diag = lax.dot_general(qs, ks, _NT, preferred_element_type=F32)
            groups.extend([off_diag[:8, :], off_diag[8:, :]])

        slabs = []
        for s in range(c):
            g0, end = (s // 8) * 8, (s // sb + 1) * sb
            p = qf[g0:end, :] * kk[s:s + 1, :] * jnp.exp2(b2[g0:end, :] - b2[s:s + 1, :])
            if s % 8:
                head = jnp.where(sub8 >= (s % 8), p[:8, :], 0.0)
                p = head if end - g0 == 8 else jnp.concatenate([head, p[8:, :]], axis=0)
            slabs.append(p)
        red = jnp.dot(jnp.concatenate(slabs, axis=0).astype(BF16), ones, preferred_element_type=F32)
        off = 0
        for s in range(c):
            for g in range(s // 8, (s // sb + 1) * (sb // 8)):
                groups[g] = jnp.where(lane8 == s, red[off:off + 8, :c], groups[g])
                off += 8
        scores = jnp.concatenate(groups, axis=0)

        o_intra = jnp.dot(scores.astype(BF16), vb, preferred_element_type=F32)
        b_last = b2[c - 1:c, :]
        qt = (qf * jnp.exp2(b2)).astype(BF16)
        kt = (kk * jnp.exp2(b_last - b2)).astype(BF16)
        upd = lax.dot_general(vb, kt, _TN, preferred_element_type=F32)
        pre.append((o_intra, qt, upd, jnp.exp2(b_last)))

    st = st_ref[...]
    for ci, (o_intra, qt, upd, dec) in enumerate(pre):
        r0 = ci * c
        o = o_intra + lax.dot_general(qt, st.astype(BF16), _NT, preferred_element_type=F32)
        st = st * dec + upd
        o = o * lax.rsqrt(jnp.mean(o * o, axis=-1, keepdims=True) + RMS_EPS)
        gg = g_ref[r0:r0 + c, :].astype(F32)
        o_ref[r0:r0 + c, :] = (o * ng * (gg * jax.nn.sigmoid(gg))).astype(o_ref.dtype)
    st_ref[...] = st


def _hgrn(proj, lb, ng, ltri, ones, weights, *, batch, seq, rb=2048):
    t = proj.shape[0]
    rb = min(rb, seq)
    nr = seq // rb
    nsteps = batch * HG_HEADS * nr
    base = 3 * SB_HEADS

    def col(k):
        return pl.BlockSpec((rb, LANES),
                            lambda bh, r: ((bh // HG_HEADS) * nr + r, base + k * HG_HEADS + bh % HG_HEADS))

    def slab(w):
        return pl.BlockSpec((w.shape[0] // nsteps, w.shape[1]), lambda bh, r: (bh * nr + r, 0))

    vec = pl.BlockSpec((None, 1, LANES), lambda bh, r: (bh % HG_HEADS, 0, 0))
    kernel = functools.partial(_hgrn_kernel, rb=rb, c=HG_CHUNK, sb=HG_SUB, nw=len(weights))
    return pl.pallas_call(
        kernel,
        grid=(batch * HG_HEADS, nr),
        in_specs=[col(0), col(1), col(2), col(3), vec, vec,
                  pl.BlockSpec((HG_CHUNK, HG_CHUNK), lambda bh, r: (0, 0)),
                  pl.BlockSpec((LANES, LANES), lambda bh, r: (0, 0)),
                  *[slab(w) for w in weights]],
        out_specs=[pl.BlockSpec((rb, LANES), lambda bh, r: ((bh // HG_HEADS) * nr + r, bh % HG_HEADS)),
                   *[slab(w) for w in weights]],
        out_shape=[jax.ShapeDtypeStruct((t, HG_HEADS * LANES), BF16),
                   *[jax.ShapeDtypeStruct(w.shape, BF16) for w in weights]],
        scratch_shapes=[pltpu.VMEM((LANES, LANES), F32)],
        compiler_params=_params(("arbitrary", "arbitrary"), 32),
        name="hgrn",
    )(proj, proj, proj, proj, lb, ng, ltri, ones, *weights)


def _layer_norm(z, g, b):
    mu = jnp.mean(z, axis=-1, keepdims=True)
    zc = z - mu
    var = jnp.mean(zc * zc, axis=-1, keepdims=True)
    return zc * lax.rsqrt(var + LN_EPS) * g + b


def _mix_kernel(a_ref, b_ref, ga0_ref, ga1_ref, gb0_ref, gb1_ref, x_ref, mod_ref, wa_ref, wb_ref,
                wo_ref, g_ref, be_ref, x1_ref, h2_ref, *, alpha):
    a = a_ref[...]
    b = b_ref[...]
    half = wa_ref.shape[1] // 2
    merged = []
    for n, (ga_ref, gb_ref) in enumerate(((ga0_ref, gb0_ref), (ga1_ref, gb1_ref))):
        pa = jnp.dot(a, wa_ref[:, n * half:(n + 1) * half], preferred_element_type=F32)
        pb = jnp.dot(b, wb_ref[:, n * half:(n + 1) * half], preferred_element_type=F32)
        m = (jax.nn.sigmoid(ga_ref[...].astype(F32)) * pa
             + jax.nn.sigmoid(gb_ref[...].astype(F32)) * pb)
        merged.append(m.astype(BF16))
    y = jnp.dot(jnp.concatenate(merged, axis=1), wo_ref[...], preferred_element_type=F32)
    gate1 = mod_ref[2:3, :]
    shift2 = mod_ref[3:4, :]
    scale2 = mod_ref[4:5, :]
    x1 = _layer_norm(alpha * x_ref[...] + (1.0 + gate1) * y, g_ref[...], be_ref[...])
    x1_ref[...] = x1
    h2_ref[...] = (x1 * (1.0 + scale2) + shift2).astype(BF16)


def _mix(a_out, b_out, proj, x2, mod, wa, wb, wo, ln_g, ln_b, *, seq, alpha, tm=512):
    t, d = x2.shape
    w = a_out.shape[1]
    tm = min(tm, seq)
    per_b = seq // tm
    gbase = (3 * SB_HEADS + 4 * HG_HEADS) * LANES // w
    row = lambda i: (i, 0)
    const = lambda i: (0, 0)

    def gate(k):
        return pl.BlockSpec((tm, w), lambda i: (i, gbase + k))

    kernel = functools.partial(_mix_kernel, alpha=alpha)
    return pl.pallas_call(
        kernel,
        grid=(t // tm,),
        in_specs=[pl.BlockSpec((tm, w), row), pl.BlockSpec((tm, w), row),
                  gate(0), gate(1), gate(2), gate(3),
                  pl.BlockSpec((tm, d), row),
                  pl.BlockSpec((None, 6, d), lambda i: (i // per_b, 0, 0)),
                  pl.BlockSpec((w, d), const, pipeline_mode=pl.Buffered(1)),
                  pl.BlockSpec((w, d), const, pipeline_mode=pl.Buffered(1)),
                  pl.BlockSpec((d, d), const, pipeline_mode=pl.Buffered(1)),
                  pl.BlockSpec((1, d), const), pl.BlockSpec((1, d), const)],
        out_specs=[pl.BlockSpec((tm, d), row), pl.BlockSpec((tm, d), row)],
        out_shape=[jax.ShapeDtypeStruct((t, d), F32), jax.ShapeDtypeStruct((t, d), BF16)],
        compiler_params=_params(("arbitrary",), 56),
        name="mix",
    )(a_out, b_out, proj, proj, proj, proj, x2, mod, wa, wb, wo, ln_g, ln_b)


def _top_rows(vals, payloads, k, n_rows):
    rid = lax.broadcasted_iota(jnp.int32, vals.shape, 0).astype(F32)
    best, rows, outs = [], [], [[] for _ in payloads]
    for _ in range(k):
        m = jnp.max(vals, axis=0, keepdims=True)
        first = jnp.min(jnp.where(vals == m, rid, float(n_rows)), axis=0, keepdims=True)
        sel = rid == first
        best.append(m)
        rows.append(first)
        for o, p in zip(outs, payloads):
            o.append(jnp.sum(jnp.where(sel, p, 0.0), axis=0, keepdims=True))
        vals = jnp.where(sel, -jnp.inf, vals)
    return best, rows, outs


def _route_kernel(h_ref, wq_ref, keys_ref, tab_ref, ee_ref, gt_ref, tab16_ref):
    tab16_ref[...] = tab_ref[...].astype(BF16)
    tb = h_ref.shape[0]
    k = PEER_TOPK
    q = jnp.dot(h_ref[...], wq_ref[...], preferred_element_type=F32).astype(BF16)
    ee, gt = [], []
    for h in range(PEER_HEADS):
        tops, idxs = [], []
        for p in range(2):
            c0 = (h * 2 + p) * LANES
            s = lax.dot_general(keys_ref[h * 2 + p], q[:, c0:c0 + LANES], _NT,
                                preferred_element_type=F32)
            best, rows, _ = _top_rows(s, [], k, N_KEYS)
            tops.append(best)
            idxs.append(rows)
        top1 = jnp.concatenate(tops[1], axis=0)
        idx1 = jnp.concatenate(idxs[1], axis=0)
        cand, ce = [], []
        for a in range(8):
            nb = k // (a + 1)
            cand.append(tops[0][a] + top1[:nb])
            ce.append(idxs[0][a] * float(N_KEYS) + idx1[:nb])
        npad = -sum(c.shape[0] for c in cand) % 8
        cand.append(jnp.full((npad, tb), -jnp.inf, F32))
        ce.append(jnp.zeros((npad, tb), F32))
        cand.append(jnp.concatenate(tops[0][8:], axis=0) + top1[:1])
        ce.append(jnp.concatenate(idxs[0][8:], axis=0) * float(N_KEYS) + idx1[:1])
        cand = jnp.concatenate(cand, axis=0)
        best, _, (be,) = _top_rows(cand, [jnp.concatenate(ce, axis=0)], k, cand.shape[0])
        e = [jnp.exp(v - best[0]) for v in best]
        denom = functools.reduce(lambda a, b: a + b, e)
        inv = 1.0 / denom
        ee.extend(be)
        gt.extend([v * inv for v in e])
    ee_ref[...] = jnp.concatenate(ee, axis=0).T
    gt_ref[...] = jnp.concatenate(gt, axis=0).T


def _route(h2, wq, keys, table, *, tb=256):
    t, d = h2.shape
    tb = min(tb, t)
    nk = PEER_HEADS * PEER_TOPK
    rows = table.shape[0] // (t // tb)
    out = jax.ShapeDtypeStruct((t, nk), F32)
    ospec = pl.BlockSpec((tb, nk), lambda i: (i, 0))
    tspec = pl.BlockSpec((rows, table.shape[1]), lambda i: (i, 0))
    return pl.pallas_call(
        _route_kernel,
        grid=(t // tb,),
        in_specs=[pl.BlockSpec((tb, d), lambda i: (i, 0)),
                  pl.BlockSpec(wq.shape, lambda i: (0, 0)),
                  pl.BlockSpec(keys.shape, lambda i: (0, 0, 0)),
                  tspec],
        out_specs=[ospec, ospec, tspec],
        out_shape=[out, out, jax.ShapeDtypeStruct(table.shape, BF16)],
        compiler_params=_params(("arbitrary",), 48),
        name="route",
    )(h2, wq, keys, table)


def _scatter_kernel(ee_ref, gt_ref, tab_ref, o_ref, tab16_ref, *, tb, grp):
    tab16_ref[...] = tab_ref[...].astype(BF16)
    sub = lax.broadcasted_iota(jnp.int32, (N_KEYS, LANES), 0).astype(F32).astype(BF16)
    one = jnp.ones((), BF16)
    zero = jnp.zeros((), BF16)

    def body(g, carry):
        t0 = pl.multiple_of(g * grp, grp)
        prods = []
        for u in range(grp):
            erow = ee_ref[pl.ds(t0 + u, 1), :]
            ihi = jnp.floor(erow * (1.0 / N_KEYS))
            irow = ihi.astype(BF16)
            jrow = (erow - ihi * float(N_KEYS)).astype(BF16)
            grow = gt_ref[pl.ds(t0 + u, 1), :].astype(BF16)
            r = jnp.where(sub == irow, grow, zero)
            cm = jnp.where(sub == jrow, one, zero)
            prods.append(lax.dot_general(r, cm, _NT, preferred_element_type=F32))
        y = jnp.swapaxes(jnp.stack(prods, axis=0), 0, 1).astype(o_ref.dtype)
        for i in range(N_KEYS):
            o_ref[pl.ds(t0, grp), i * N_KEYS:(i + 1) * N_KEYS] = y[i]
        return carry

    lax.fori_loop(0, tb // grp, body, 0)


def _scatter(ee, gt, table, *, tb=256, grp=16):
    t, nk = ee.shape
    tb = min(tb, t)
    rows = table.shape[0] // (t // tb)
    spec = pl.BlockSpec((tb, nk), lambda i: (i, 0))
    tspec = pl.BlockSpec((rows, table.shape[1]), lambda i: (i, 0))
    kernel = functools.partial(_scatter_kernel, tb=tb, grp=grp)
    return pl.pallas_call(
        kernel,
        grid=(t // tb,),
        in_specs=[spec, spec, tspec],
        out_specs=[pl.BlockSpec((tb, N_KEYS * N_KEYS), lambda i: (i, 0)), tspec],
        out_shape=[jax.ShapeDtypeStruct((t, N_KEYS * N_KEYS), BF16),
                   jax.ShapeDtypeStruct(table.shape, BF16)],
        compiler_params=_params(("arbitrary",), 48),
        name="scatter",
    )(ee, gt, table)


def _peer_kernel(h_ref, u_ref, v_ref, gm_ref, x1_ref, mod_ref, g_ref, be_ref, o_ref, *, alpha):
    j = pl.program_id(1)

    @pl.when(j == 0)
    def _():
        o_ref[...] = jnp.zeros_like(o_ref)

    act = lax.dot_general(h_ref[...], u_ref[...], _NT, preferred_element_type=F32)
    gelu = 0.5 * act * (1.0 + lax.erf(act * (1.0 / math.sqrt(2.0))))
    w = (gm_ref[...].astype(F32) * gelu).astype(BF16)
    o_ref[...] += jnp.dot(w, v_ref[...], preferred_element_type=F32)

    @pl.when(j == pl.num_programs(1) - 1)
    def _():
        gate2 = mod_ref[5:6, :]
        o_ref[...] = _layer_norm(alpha * x1_ref[...] + (1.0 + gate2) * o_ref[...],
                                 g_ref[...], be_ref[...])


def _peer(h2, u, v, gm, x1, mod, ln_g, ln_b, *, seq, alpha, tb=1024, eb=512, vmem_mib=62):
    t, d = h2.shape
    ne = v.shape[0]
    tb = min(tb, seq)
    per_b = seq // tb
    row = lambda i, j: (i, 0)
    const = lambda i, j: (0, 0)
    kernel = functools.partial(_peer_kernel, alpha=alpha)
    return pl.pallas_call(
        kernel,
        grid=(t // tb, ne // eb),
        in_specs=[pl.BlockSpec((tb, d), row),
                  pl.BlockSpec((eb, d), lambda i, j: (j, 0)),
                  pl.BlockSpec((eb, d), lambda i, j: (j, 0)),
                  pl.BlockSpec((tb, eb), lambda i, j: (i, j)),
                  pl.BlockSpec((tb, d), row),
                  pl.BlockSpec((None, 6, d), lambda i, j: (i // per_b, 0, 0)),
                  pl.BlockSpec((1, d), const), pl.BlockSpec((1, d), const)],
        out_specs=pl.BlockSpec((tb, d), row),
        out_shape=jax.ShapeDtypeStruct((t, d), F32),
        compiler_params=_params(("arbitrary", "arbitrary"), vmem_mib),
        name="peer",
    )(h2, u, v, gm, x1, mod, ln_g, ln_b)


def kernel(x, c, w_ada, b_ada, w_in, hg_lb_logits, hg_norm_g, w_branch_a, w_branch_b, w_out,
           ln1_g, ln1_b, peer_wq, peer_subkeys, peer_u, peer_v, ln2_g, ln2_b):
    batch, seq, d = x.shape
    depth = w_ada.shape[0]
    alpha = (2.0 * depth) ** 0.25
    lower_bounds = jnp.cumsum(jax.nn.softmax(hg_lb_logits.astype(F32), axis=0), axis=0)

    r = lax.broadcasted_iota(jnp.int32, (LANES, LANES), 0)
    cc = lax.broadcasted_iota(jnp.int32, (LANES, LANES), 1)
    rb = lax.broadcasted_iota(jnp.int32, (SB_KBLOCK, SB_KBLOCK), 0)
    cb = lax.broadcasted_iota(jnp.int32, (SB_KBLOCK, SB_KBLOCK), 1)
    tri = jnp.concatenate([(rb > cb), jnp.ones((SB_KBLOCK, LANES), bool)], axis=1).astype(BF16)
    ltri = (r[:HG_CHUNK, :HG_CHUNK] >= cc[:HG_CHUNK, :HG_CHUNK]).astype(BF16)
    ones = jnp.ones((LANES, LANES), BF16)

    c_pad = jnp.zeros((8, d), F32).at[:batch].set(c)
    xt = x.reshape(batch * seq, d)
    for layer in range(depth):
        mod = _ada(c_pad, w_ada[layer:layer + 1], b_ada[layer:layer + 1])[:batch].reshape(batch, 6, d)
        proj = _inproj(xt, mod, w_in[layer].astype(BF16), seq=seq)
        a_out = _sbattn(proj, tri, batch=batch, seq=seq)
        b_out, wo16, wq16, wa16, wb16 = _hgrn(
            proj, lower_bounds[layer].reshape(HG_HEADS, 1, LANES),
            hg_norm_g[layer].reshape(HG_HEADS, 1, LANES), ltri, ones,
            (w_out[layer], peer_wq[layer], w_branch_a[layer], w_branch_b[layer]), batch=batch, seq=seq)
        x1, h2 = _mix(a_out, b_out, proj, xt, mod, wa16, wb16, wo16,
                      ln1_g[layer:layer + 1], ln1_b[layer:layer + 1], seq=seq, alpha=alpha)
        keys = peer_subkeys[layer].reshape(PEER_HEADS * 2, N_KEYS, LANES).astype(BF16)
        ee, gt, u16 = _route(h2, wq16, keys, peer_u[layer])
        gm, v16 = _scatter(ee, gt, peer_v[layer])
        xt = _peer(h2, u16, v16, gm, x1, mod,
                   ln2_g[layer:layer + 1], ln2_b[layer:layer + 1], seq=seq, alpha=alpha)
    return xt.reshape(batch, seq, d)
```

```python
import functools
import math

import jax
import jax.numpy as jnp
from jax import lax
from jax.experimental import pallas as pl
from jax.experimental.pallas import tpu as pltpu

F32 = jnp.float32
BF16 = jnp.bfloat16

LANES = 128
SB_HEADS = 8
SB_QBLOCK = 1024
SB_KBLOCK = 256
HG_HEADS = 8
HG_CHUNK = 64
HG_SUB = 16
LOG2E = 1.4426950408889634
PEER_HEADS = 8
PEER_TOPK = 16
N_KEYS = 128
RMS_EPS = 1e-6
LN_EPS = 1e-5
MIB = 1024 * 1024

_NT = (((1,), (1,)), ((), ()))
_TN = (((0,), (0,)), ((), ()))


def _params(semantics, vmem_mib):
    return pltpu.CompilerParams(dimension_semantics=semantics, vmem_limit_bytes=vmem_mib * MIB)


def _ada_kernel(c_ref, w_ref, b_ref, o_ref):
    c = c_ref[...]
    cond = c * jax.nn.sigmoid(c)
    o_ref[...] = jnp.dot(cond.astype(BF16), w_ref[...].astype(BF16),
                         preferred_element_type=F32) + b_ref[...]


def _ada(c_pad, w_ada, b_ada, *, tn=1024):
    rows, d = c_pad.shape
    n = w_ada.shape[-1]
    return pl.pallas_call(
        _ada_kernel,
        grid=(n // tn,),
        in_specs=[pl.BlockSpec((rows, d), lambda j: (0, 0)),
                  pl.BlockSpec((None, d, tn), lambda j: (0, 0, j)),
                  pl.BlockSpec((1, tn), lambda j: (0, j))],
        out_specs=pl.BlockSpec((rows, tn), lambda j: (0, j)),
        out_shape=jax.ShapeDtypeStruct((rows, n), F32),
        compiler_params=_params(("arbitrary",), 40),
        name="ada",
    )(c_pad, w_ada, b_ada)


def _inproj_kernel(x_ref, mod_ref, w_ref, o_ref, h_scr):
    @pl.when(pl.program_id(1) == 0)
    def _():
        shift = mod_ref[0:1, :]
        scale = mod_ref[1:2, :]
        h_scr[...] = (x_ref[...] * (1.0 + scale) + shift).astype(BF16)

    o_ref[...] = jnp.dot(h_scr[...], w_ref[...], preferred_element_type=F32).astype(BF16)


def _inproj(x2, mod, w_in, *, seq, tm=1024, tn=2816, vmem_mib=60):
    t, d = x2.shape
    n = w_in.shape[1]
    tm = min(tm, seq)
    per_b = seq // tm
    return pl.pallas_call(
        _inproj_kernel,
        grid=(t // tm, n // tn),
        in_specs=[pl.BlockSpec((tm, d), lambda i, j: (i, 0)),
                  pl.BlockSpec((None, 6, d), lambda i, j: (i // per_b, 0, 0)),
                  pl.BlockSpec((d, tn), lambda i, j: (0, j))],
        out_specs=pl.BlockSpec((tm, tn), lambda i, j: (i, j)),
        out_shape=jax.ShapeDtypeStruct((t, n), BF16),
        scratch_shapes=[pltpu.VMEM((tm, d), BF16)],
        compiler_params=_params(("arbitrary", "arbitrary"), vmem_mib),
        name="inproj",
    )(x2, mod, w_in)


def _sb_kernel(q_ref, k_ref, v_ref, tri_ref, o_ref, *, tq, tk, scale):
    qi = pl.program_id(1)
    nd = tq // tk
    q = (q_ref[...].astype(F32) * scale).astype(BF16)
    tri = tri_ref[...]

    def tile(j, acc, out, r0=None):
        ks = pl.multiple_of(j * tk, tk)
        z = lax.dot_general(q[r0:], k_ref[pl.ds(ks, tk), :], _NT, preferred_element_type=F32)
        lp = jnp.log(1.0 + jnp.exp2(jnp.abs(z) * (-LOG2E)))
        logb = jnp.minimum(z, 0.0) - lp
        l1mb = logb - z
        if r0 is not None:
            mask = (lax.broadcasted_iota(jnp.int32, z.shape, 1)
                    < lax.broadcasted_iota(jnp.int32, z.shape, 0))
            l1mb = jnp.where(mask, l1mb, 0.0)
            logb = jnp.where(mask, logb, -1e30)
        cs = jnp.dot(l1mb.astype(BF16), tri, preferred_element_type=F32)
        w = jnp.exp(logb + cs[:, :tk] + jnp.concatenate([acc[r0:]] * (tk // LANES), axis=1))
        pv = jnp.dot(w.astype(BF16), v_ref[pl.ds(ks, tk), :], preferred_element_type=F32)
        if r0:
            pad = jnp.zeros((r0, LANES), F32)
            return acc + jnp.concatenate([pad, cs[:, tk:]], axis=0), out + jnp.concatenate([pad, pv], axis=0)
        return acc + cs[:, tk:], out + pv

    acc = jnp.zeros((tq, LANES), F32)
    out = jnp.zeros((tq, LANES), F32)
    for dd in reversed(range(nd)):
        acc, out = tile(qi * nd + dd, acc, out, dd * tk)

    def steps(j, n, carry):
        a, o = carry
        for u in range(n):
            a, o = tile(j - u, a, o, None)
        return a, o

    pairs = qi // 2
    acc, out = lax.fori_loop(
        0, pairs, lambda i, c: steps((qi - 2 * i) * nd - 1, 2 * nd, c), (acc, out))
    acc, out = lax.fori_loop(
        0, qi - 2 * pairs, lambda i, c: steps(nd - 1, nd, c), (acc, out))
    o_ref[...] = out.astype(o_ref.dtype)


def _sbattn(proj, tri, *, batch, seq, tq=SB_QBLOCK, tk=SB_KBLOCK):
    t = proj.shape[0]
    tq = min(tq, seq)
    nq = seq // tq
    kernel = functools.partial(_sb_kernel, tq=tq, tk=tk, scale=1.0 / math.sqrt(LANES))
    return pl.pallas_call(
        kernel,
        grid=(batch * SB_HEADS, nq),
        in_specs=[pl.BlockSpec((tq, LANES), lambda bh, qi: ((bh // SB_HEADS) * nq + qi, bh % SB_HEADS)),
                  pl.BlockSpec((seq, LANES), lambda bh, qi: (bh // SB_HEADS, SB_HEADS + bh % SB_HEADS)),
                  pl.BlockSpec((seq, LANES), lambda bh, qi: (bh // SB_HEADS, 2 * SB_HEADS + bh % SB_HEADS)),
                  pl.BlockSpec((tk, tk + LANES), lambda bh, qi: (0, 0))],
        out_specs=pl.BlockSpec((tq, LANES), lambda bh, qi: ((bh // SB_HEADS) * nq + qi, bh % SB_HEADS)),
        out_shape=jax.ShapeDtypeStruct((t, SB_HEADS * LANES), BF16),
        compiler_params=_params(("arbitrary", "arbitrary"), 40),
        name="sbattn",
    )(proj, proj, proj, tri)


def _hgrn_kernel(q_ref, f_ref, i_ref, g_ref, lb_ref, ng_ref, ltri_ref, ones_ref, *refs, rb, c, sb, nw):
    o_ref, st_ref = refs[nw], refs[2 * nw + 1]
    for w_ref, w16_ref in zip(refs[:nw], refs[nw + 1:2 * nw + 1]):
        w16_ref[...] = w_ref[...].astype(BF16)

    @pl.when(pl.program_id(1) == 0)
    def _():
        st_ref[...] = jnp.zeros_like(st_ref)

    lb = lb_ref[...]
    ng = ng_ref[...]
    ltri = ltri_ref[...]
    ones = ones_ref[...]
    lane8 = lax.broadcasted_iota(jnp.int32, (8, c), 1)
    sub8 = lax.broadcasted_iota(jnp.int32, (8, LANES), 0)

    pre = []
    for ci in range(rb // c):
        r0 = ci * c
        qf = q_ref[r0:r0 + c, :].astype(F32)
        ff = f_ref[r0:r0 + c, :].astype(F32)
        vb = i_ref[r0:r0 + c, :]
        fg = lb + (1.0 - lb) * jax.nn.sigmoid(ff)
        lf = jnp.log(fg)
        kk = 1.0 - fg
        hi = lf.astype(BF16)
        lo = (lf - hi.astype(F32)).astype(BF16)
        b = (jnp.dot(ltri, hi, preferred_element_type=F32)
             + jnp.dot(ltri, lo, preferred_element_type=F32))

        b2 = b * LOG2E

        groups = [jnp.zeros((8, c), F32), jnp.zeros((8, c), F32)]
        for i in range(1, c // sb):
            r = b2[i * sb - 1:i * sb, :]
            qs = (qf[i * sb:(i + 1) * sb, :] * jnp.exp2(b2[i * sb:(i + 1) * sb, :] - r)).astype(BF16)
            ks = (kk[:i * sb, :] * jnp.exp2(r - b2[:i * sb, :])).astype(BF16)
            ks = jnp.concatenate([ks, jnp.zeros((c - i * sb, LANES), BF16)], axis=0)
            off_diag = lax.dot_general(qs, ks, _NT, preferred_element_type=F32)
            groups.extend([off_diag[:8, :], off_diag[8:, :]])

        slabs = []
        for s in range(c):
            g0, end = (s // 8) * 8, (s // sb + 1) * sb
            p = qf[g0:end, :] * kk[s:s + 1, :] * jnp.exp2(b2[g0:end, :] - b2[s:s + 1, :])
            if s % 8:
                head = jnp.where(sub8 >= (s % 8), p[:8, :], 0.0)
                p = head if end - g0 == 8 else jnp.concatenate([head, p[8:, :]], axis=0)
            slabs.append(p)
        red = jnp.dot(jnp.concatenate(slabs, axis=0).astype(BF16), ones, preferred_element_type=F32)
        off = 0
        for s in range(c):
            for g in range(s // 8, (s // sb + 1) * (sb // 8)):
                groups[g] = jnp.where(lane8 == s, red[off:off + 8, :c], groups[g])
                off += 8
        scores = jnp.concatenate(groups, axis=0)

        o_intra = jnp.dot(scores.astype(BF16), vb, preferred_element_type=F32)
        b_last = b2[c - 1:c, :]
        qt = (qf * jnp.exp2(b2)).astype(BF16)
        kt = (kk * jnp.exp2(b_last - b2)).astype(BF16)
        upd = lax.dot_general(vb, kt, _TN, preferred_element_type=F32)
        pre.append((o_intra, qt, upd, jnp.exp2(b_last)))

    st = st_ref[...]
    for ci, (o_intra, qt, upd, dec) in enumerate(pre):
        r0 = ci * c
        o = o_intra + lax.dot_general(qt, st.astype(BF16), _NT, preferred_element_type=F32)
        st = st * dec + upd
        o = o * lax.rsqrt(jnp.mean(o * o, axis=-1, keepdims=True) + RMS_EPS)
        gg = g_ref[r0:r0 + c, :].astype(F32)
        o_ref[r0:r0 + c, :] = (o * ng * (gg * jax.nn.sigmoid(gg))).astype(o_ref.dtype)
    st_ref[...] = st


def _hgrn(proj, lb, ng, ltri, ones, weights, *, batch, seq, rb=4096):
    t = proj.shape[0]
    rb = min(rb, seq)
    nr = seq // rb
    nsteps = batch * HG_HEADS * nr
    base = 3 * SB_HEADS

    def col(k):
        return pl.BlockSpec((rb, LANES),
                            lambda bh, r: ((bh // HG_HEADS) * nr + r, base + k * HG_HEADS + bh % HG_HEADS))

    def slab(w):
        return pl.BlockSpec((w.shape[0] // nsteps, w.shape[1]), lambda bh, r: (bh * nr + r, 0))

    vec = pl.BlockSpec((None, 1, LANES), lambda bh, r: (bh % HG_HEADS, 0, 0))
    kernel = functools.partial(_hgrn_kernel, rb=rb, c=HG_CHUNK, sb=HG_SUB, nw=len(weights))
    return pl.pallas_call(
        kernel,
        grid=(batch * HG_HEADS, nr),
        in_specs=[col(0), col(1), col(2), col(3), vec, vec,
                  pl.BlockSpec((HG_CHUNK, HG_CHUNK), lambda bh, r: (0, 0)),
                  pl.BlockSpec((LANES, LANES), lambda bh, r: (0, 0)),
                  *[slab(w) for w in weights]],
        out_specs=[pl.BlockSpec((rb, LANES), lambda bh, r: ((bh // HG_HEADS) * nr + r, bh % HG_HEADS)),
                   *[slab(w) for w in weights]],
        out_shape=[jax.ShapeDtypeStruct((t, HG_HEADS * LANES), BF16),
                   *[jax.ShapeDtypeStruct(w.shape, BF16) for w in weights]],
        scratch_shapes=[pltpu.VMEM((LANES, LANES), F32)],
        compiler_params=_params(("arbitrary", "arbitrary"), 32),
        name="hgrn",
    )(proj, proj, proj, proj, lb, ng, ltri, ones, *weights)


def _layer_norm(z, g, b):
    mu = jnp.mean(z, axis=-1, keepdims=True)
    zc = z - mu
    var = jnp.mean(zc * zc, axis=-1, keepdims=True)
    return zc * lax.rsqrt(var + LN_EPS) * g + b


def _mix_kernel(a_ref, b_ref, ga0_ref, ga1_ref, gb0_ref, gb1_ref, x_ref, mod_ref, wa_ref, wb_ref,
                wo_ref, g_ref, be_ref, x1_ref, h2_ref, *, alpha):
    a = a_ref[...]
    b = b_ref[...]
    half = wa_ref.shape[1] // 2
    merged = []
    for n, (ga_ref, gb_ref) in enumerate(((ga0_ref, gb0_ref), (ga1_ref, gb1_ref))):
        pa = jnp.dot(a, wa_ref[:, n * half:(n + 1) * half], preferred_element_type=F32)
        pb = jnp.dot(b, wb_ref[:, n * half:(n + 1) * half], preferred_element_type=F32)
        m = (jax.nn.sigmoid(ga_ref[...].astype(F32)) * pa
             + jax.nn.sigmoid(gb_ref[...].astype(F32)) * pb)
        merged.append(m.astype(BF16))
    y = jnp.dot(jnp.concatenate(merged, axis=1), wo_ref[...], preferred_element_type=F32)
    gate1 = mod_ref[2:3, :]
    shift2 = mod_ref[3:4, :]
    scale2 = mod_ref[4:5, :]
    x1 = _layer_norm(alpha * x_ref[...] + (1.0 + gate1) * y, g_ref[...], be_ref[...])
    x1_ref[...] = x1
    h2_ref[...] = (x1 * (1.0 + scale2) + shift2).astype(BF16)


def _mix(a_out, b_out, proj, x2, mod, wa, wb, wo, ln_g, ln_b, *, seq, alpha, tm=512):
    t, d = x2.shape
    w = a_out.shape[1]
    tm = min(tm, seq)
    per_b = seq // tm
    gbase = (3 * SB_HEADS + 4 * HG_HEADS) * LANES // w
    row = lambda i: (i, 0)
    const = lambda i: (0, 0)

    def gate(k):
        return pl.BlockSpec((tm, w), lambda i: (i, gbase + k))

    kernel = functools.partial(_mix_kernel, alpha=alpha)
    return pl.pallas_call(
        kernel,
        grid=(t // tm,),
        in_specs=[pl.BlockSpec((tm, w), row), pl.BlockSpec((tm, w), row),
                  gate(0), gate(1), gate(2), gate(3),
                  pl.BlockSpec((tm, d), row),
                  pl.BlockSpec((None, 6, d), lambda i: (i // per_b, 0, 0)),
                  pl.BlockSpec((w, d), const, pipeline_mode=pl.Buffered(1)),
                  pl.BlockSpec((w, d), const, pipeline_mode=pl.Buffered(1)),
                  pl.BlockSpec((d, d), const, pipeline_mode=pl.Buffered(1)),
                  pl.BlockSpec((1, d), const), pl.BlockSpec((1, d), const)],
        out_specs=[pl.BlockSpec((tm, d), row), pl.BlockSpec((tm, d), row)],
        out_shape=[jax.ShapeDtypeStruct((t, d), F32), jax.ShapeDtypeStruct((t, d), BF16)],
        compiler_params=_params(("arbitrary",), 56),
        name="mix",
    )(a_out, b_out, proj, proj, proj, proj, x2, mod, wa, wb, wo, ln_g, ln_b)


def _top_rows(vals, payloads, k, n_rows):
    rid = lax.broadcasted_iota(jnp.int32, vals.shape, 0).astype(F32)
    best, rows, outs = [], [], [[] for _ in payloads]
    for _ in range(k):
        m = jnp.max(vals, axis=0, keepdims=True)
        first = jnp.min(jnp.where(vals == m, rid, float(n_rows)), axis=0, keepdims=True)
        sel = rid == first
        best.append(m)
        rows.append(first)
        for o, p in zip(outs, payloads):
            o.append(jnp.sum(jnp.where(sel, p, 0.0), axis=0, keepdims=True))
        vals = jnp.where(sel, -jnp.inf, vals)
    return best, rows, outs


def _route_kernel(h_ref, wq_ref, keys_ref, tab_ref, ee_ref, gt_ref, tab16_ref):
    tab16_ref[...] = tab_ref[...].astype(BF16)
    tb = h_ref.shape[0]
    k = PEER_TOPK
    q = jnp.dot(h_ref[...], wq_ref[...], preferred_element_type=F32).astype(BF16)
    ee, gt = [], []
    for h in range(PEER_HEADS):
        tops, idxs = [], []
        for p in range(2):
            c0 = (h * 2 + p) * LANES
            s = lax.dot_general(keys_ref[h * 2 + p], q[:, c0:c0 + LANES], _NT,
                                preferred_element_type=F32)
            best, rows, _ = _top_rows(s, [], k, N_KEYS)
            tops.append(best)
            idxs.append(rows)
        top1 = jnp.concatenate(tops[1], axis=0)
        idx1 = jnp.concatenate(idxs[1], axis=0)
        cand, ce = [], []
        for a in range(8):
            nb = k // (a + 1)
            cand.append(tops[0][a] + top1[:nb])
            ce.append(idxs[0][a] * float(N_KEYS) + idx1[:nb])
        npad = -sum(c.shape[0] for c in cand) % 8
        cand.append(jnp.full((npad, tb), -jnp.inf, F32))
        ce.append(jnp.zeros((npad, tb), F32))
        cand.append(jnp.concatenate(tops[0][8:], axis=0) + top1[:1])
        ce.append(jnp.concatenate(idxs[0][8:], axis=0) * float(N_KEYS) + idx1[:1])
        cand = jnp.concatenate(cand, axis=0)
        best, _, (be,) = _top_rows(cand, [jnp.concatenate(ce, axis=0)], k, cand.shape[0])
        e = [jnp.exp(v - best[0]) for v in best]
        denom = functools.reduce(lambda a, b: a + b, e)
        inv = 1.0 / denom
        ee.extend(be)
        gt.extend([v * inv for v in e])
    ee_ref[...] = jnp.concatenate(ee, axis=0).T
    gt_ref[...] = jnp.concatenate(gt, axis=0).T


def _route(h2, wq, keys, table, *, tb=256):
    t, d = h2.shape
    tb = min(tb, t)
    nk = PEER_HEADS * PEER_TOPK
    rows = table.shape[0] // (t // tb)
    out = jax.ShapeDtypeStruct((t, nk), F32)
    ospec = pl.BlockSpec((tb, nk), lambda i: (i, 0))
    tspec = pl.BlockSpec((rows, table.shape[1]), lambda i: (i, 0))
    return pl.pallas_call(
        _route_kernel,
        grid=(t // tb,),
        in_specs=[pl.BlockSpec((tb, d), lambda i: (i, 0)),
                  pl.BlockSpec(wq.shape, lambda i: (0, 0)),
                  pl.BlockSpec(keys.shape, lambda i: (0, 0, 0)),
                  tspec],
        out_specs=[ospec, ospec, tspec],
        out_shape=[out, out, jax.ShapeDtypeStruct(table.shape, BF16)],
        compiler_params=_params(("arbitrary",), 48),
        name="route",
    )(h2, wq, keys, table)


def _scatter_kernel(ee_ref, gt_ref, tab_ref, o_ref, tab16_ref, *, tb, grp):
    tab16_ref[...] = tab_ref[...].astype(BF16)
    sub = lax.broadcasted_iota(jnp.int32, (N_KEYS, LANES), 0).astype(F32).astype(BF16)
    one = jnp.ones((), BF16)
    zero = jnp.zeros((), BF16)

    def body(g, carry):
        t0 = pl.multiple_of(g * grp, grp)
        prods = []
        for u in range(grp):
            erow = ee_ref[pl.ds(t0 + u, 1), :]
            ihi = jnp.floor(erow * (1.0 / N_KEYS))
            irow = ihi.astype(BF16)
            jrow = (erow - ihi * float(N_KEYS)).astype(BF16)
            grow = gt_ref[pl.ds(t0 + u, 1), :].astype(BF16)
            r = jnp.where(sub == irow, grow, zero)
            cm = jnp.where(sub == jrow, one, zero)
            prods.append(lax.dot_general(r, cm, _NT, preferred_element_type=F32))
        y = jnp.swapaxes(jnp.stack(prods, axis=0), 0, 1).astype(o_ref.dtype)
        for i in range(N_KEYS):
            o_ref[pl.ds(t0, grp), i * N_KEYS:(i + 1) * N_KEYS] = y[i]
        return carry

    lax.fori_loop(0, tb // grp, body, 0)


def _scatter(ee, gt, table, *, tb=256, grp=16):
    t, nk = ee.shape
    tb = min(tb, t)
    rows = table.shape[0] // (t // tb)
    spec = pl.BlockSpec((tb, nk), lambda i: (i, 0))
    tspec = pl.BlockSpec((rows, table.shape[1]), lambda i: (i, 0))
    kernel = functools.partial(_scatter_kernel, tb=tb, grp=grp)
    return pl.pallas_call(
        kernel,
        grid=(t // tb,),
        in_specs=[spec, spec, tspec],
        out_specs=[pl.BlockSpec((tb, N_KEYS * N_KEYS), lambda i: (i, 0)), tspec],
        out_shape=[jax.ShapeDtypeStruct((t, N_KEYS * N_KEYS), BF16),
                   jax.ShapeDtypeStruct(table.shape, BF16)],
        compiler_params=_params(("arbitrary",), 48),
        name="scatter",
    )(ee, gt, table)


def _peer_kernel(h_ref, u_ref, v_ref, gm_ref, x1_ref, mod_ref, g_ref, be_ref, o_ref, *, alpha):
    j = pl.program_id(1)

    @pl.when(j == 0)
    def _():
        o_ref[...] = jnp.zeros_like(o_ref)

    act = lax.dot_general(h_ref[...], u_ref[...], _NT, preferred_element_type=F32)
    gelu = 0.5 * act * (1.0 + lax.erf(act * (1.0 / math.sqrt(2.0))))
    w = (gm_ref[...].astype(F32) * gelu).astype(BF16)
    o_ref[...] += jnp.dot(w, v_ref[...], preferred_element_type=F32)

    @pl.when(j == pl.num_programs(1) - 1)
    def _():
        gate2 = mod_ref[5:6, :]
        o_ref[...] = _layer_norm(alpha * x1_ref[...] + (1.0 + gate2) * o_ref[...],
                                 g_ref[...], be_ref[...])


def _peer(h2, u, v, gm, x1, mod, ln_g, ln_b, *, seq, alpha, tb=1024, eb=512, vmem_mib=62):
    t, d = h2.shape
    ne = v.shape[0]
    tb = min(tb, seq)
    per_b = seq // tb
    row = lambda i, j: (i, 0)
    const = lambda i, j: (0, 0)
    kernel = functools.partial(_peer_kernel, alpha=alpha)
    return pl.pallas_call(
        kernel,
        grid=(t // tb, ne // eb),
        in_specs=[pl.BlockSpec((tb, d), row),
                  pl.BlockSpec((eb, d), lambda i, j: (j, 0)),
                  pl.BlockSpec((eb, d), lambda i, j: (j, 0)),
                  pl.BlockSpec((tb, eb), lambda i, j: (i, j)),
                  pl.BlockSpec((tb, d), row),
                  pl.BlockSpec((None, 6, d), lambda i, j: (i // per_b, 0, 0)),
                  pl.BlockSpec((1, d), const), pl.BlockSpec((1, d), const)],
        out_specs=pl.BlockSpec((tb, d), row),
        out_shape=jax.ShapeDtypeStruct((t, d), F32),
        compiler_params=_params(("arbitrary", "arbitrary"), vmem_mib),
        name="peer",
    )(h2, u, v, gm, x1, mod, ln_g, ln_b)


def kernel(x, c, w_ada, b_ada, w_in, hg_lb_logits, hg_norm_g, w_branch_a, w_branch_b, w_out,
           ln1_g, ln1_b, peer_wq, peer_subkeys, peer_u, peer_v, ln2_g, ln2_b):
    batch, seq, d = x.shape
    depth = w_ada.shape[0]
    alpha = (2.0 * depth) ** 0.25
    lower_bounds = jnp.cumsum(jax.nn.softmax(hg_lb_logits.astype(F32), axis=0), axis=0)

    r = lax.broadcasted_iota(jnp.int32, (LANES, LANES), 0)
    cc = lax.broadcasted_iota(jnp.int32, (LANES, LANES), 1)
    rb = lax.broadcasted_iota(jnp.int32, (SB_KBLOCK, SB_KBLOCK), 0)
    cb = lax.broadcasted_iota(jnp.int32, (SB_KBLOCK, SB_KBLOCK), 1)
    tri = jnp.concatenate([(rb > cb), jnp.ones((SB_KBLOCK, LANES), bool)], axis=1).astype(BF16)
    ltri = (r[:HG_CHUNK, :HG_CHUNK] >= cc[:HG_CHUNK, :HG_CHUNK]).astype(BF16)
    ones = jnp.ones((LANES, LANES), BF16)

    c_pad = jnp.zeros((8, d), F32).at[:batch].set(c)
    xt = x.reshape(batch * seq, d)
    for layer in range(depth):
        mod = _ada(c_pad, w_ada[layer:layer + 1], b_ada[layer:layer + 1])[:batch].reshape(batch, 6, d)
        proj = _inproj(xt, mod, w_in[layer].astype(BF16), seq=seq)
        a_out = _sbattn(proj, tri, batch=batch, seq=seq)
        b_out, wo16, wq16, wa16, wb16 = _hgrn(
            proj, lower_bounds[layer].reshape(HG_HEADS, 1, LANES),
            hg_norm_g[layer].reshape(HG_HEADS, 1, LANES), ltri, ones,
            (w_out[layer], peer_wq[layer], w_branch_a[layer], w_branch_b[layer]), batch=batch, seq=seq)
        x1, h2 = _mix(a_out, b_out, proj, xt, mod, wa16, wb16, wo16,
                      ln1_g[layer:layer + 1], ln1_b[layer:layer + 1], seq=seq, alpha=alpha)
        keys = peer_subkeys[layer].reshape(PEER_HEADS * 2, N_KEYS, LANES).astype(BF16)
        ee, gt, u16 = _route(h2, wq16, keys, peer_u[layer])
        gm, v16 = _scatter(ee, gt, peer_v[layer])
        xt = _peer(h2, u16, v16, gm, x1, mod,
                   ln2_g[layer:layer + 1], ln2_b[layer:layer + 1], seq=seq, alpha=alpha)
    return xt.reshape(batch, seq, d)
```

```python
import functools
import math

import jax
import jax.numpy as jnp
from jax import lax
from jax.experimental import pallas as pl
from jax.experimental.pallas import tpu as pltpu

F32 = jnp.float32
BF16 = jnp.bfloat16

LANES = 128
SB_HEADS = 8
SB_QBLOCK = 1024
SB_KBLOCK = 256
HG_HEADS = 8
HG_CHUNK = 64
HG_SUB = 16
LOG2E = 1.4426950408889634
PEER_HEADS = 8
PEER_TOPK = 16
N_KEYS = 128
RMS_EPS = 1e-6
LN_EPS = 1e-5
MIB = 1024 * 1024

_NT = (((1,), (1,)), ((), ()))
_TN = (((0,), (0,)), ((), ()))


V7X_VMEM_MIB = 64


def _params(semantics, vmem_mib):
    assert vmem_mib < V7X_VMEM_MIB
    return pltpu.CompilerParams(dimension_semantics=semantics, vmem_limit_bytes=vmem_mib * MIB)


def _ada_kernel(c_ref, w_ref, b_ref, o_ref):
    c = c_ref[...]
    cond = c * jax.nn.sigmoid(c)
    o_ref[...] = jnp.dot(cond.astype(BF16), w_ref[...].astype(BF16),
                         preferred_element_type=F32) + b_ref[...]


def _ada(c_pad, w_ada, b_ada, *, tn=1024):
    rows, d = c_pad.shape
    n = w_ada.shape[-1]
    return pl.pallas_call(
        _ada_kernel,
        grid=(n // tn,),
        in_specs=[pl.BlockSpec((rows, d), lambda j: (0, 0)),
                  pl.BlockSpec((None, d, tn), lambda j: (0, 0, j)),
                  pl.BlockSpec((1, tn), lambda j: (0, j))],
        out_specs=pl.BlockSpec((rows, tn), lambda j: (0, j)),
        out_shape=jax.ShapeDtypeStruct((rows, n), F32),
        compiler_params=_params(("arbitrary",), 40),
        name="ada",
    )(c_pad, w_ada, b_ada)


def _inproj_kernel(x_ref, mod_ref, w_ref, o_ref, h_scr):
    @pl.when(pl.program_id(1) == 0)
    def _():
        shift = mod_ref[0:1, :]
        scale = mod_ref[1:2, :]
        h_scr[...] = (x_ref[...] * (1.0 + scale) + shift).astype(BF16)

    o_ref[...] = jnp.dot(h_scr[...], w_ref[...], preferred_element_type=F32).astype(BF16)


def _inproj(x2, mod, w_in, *, seq, tm=1024, tn=2816, vmem_mib=60):
    t, d = x2.shape
    n = w_in.shape[1]
    tm = min(tm, seq)
    per_b = seq // tm
    return pl.pallas_call(
        _inproj_kernel,
        grid=(t // tm, n // tn),
        in_specs=[pl.BlockSpec((tm, d), lambda i, j: (i, 0)),
                  pl.BlockSpec((None, 6, d), lambda i, j: (i // per_b, 0, 0)),
                  pl.BlockSpec((d, tn), lambda i, j: (0, j))],
        out_specs=pl.BlockSpec((tm, tn), lambda i, j: (i, j)),
        out_shape=jax.ShapeDtypeStruct((t, n), BF16),
        scratch_shapes=[pltpu.VMEM((tm, d), BF16)],
        compiler_params=_params(("arbitrary", "arbitrary"), vmem_mib),
        name="inproj",
    )(x2, mod, w_in)


def _sb_kernel(q_ref, k_ref, v_ref, tri_ref, o_ref, *, tq, tk, scale):
    qi = pl.program_id(1)
    nd = tq // tk
    q = (q_ref[...].astype(F32) * scale).astype(BF16)
    tri = tri_ref[...]

    def tile(j, acc, out, r0=None):
        ks = pl.multiple_of(j * tk, tk)
        z = lax.dot_general(q[r0:], k_ref[pl.ds(ks, tk), :], _NT, preferred_element_type=F32)
        lp = jnp.log(1.0 + jnp.exp2(jnp.abs(z) * (-LOG2E)))
        logb = jnp.minimum(z, 0.0) - lp
        l1mb = logb - z
        if r0 is not None:
            mask = (lax.broadcasted_iota(jnp.int32, z.shape, 1)
                    < lax.broadcasted_iota(jnp.int32, z.shape, 0))
            l1mb = jnp.where(mask, l1mb, 0.0)
            logb = jnp.where(mask, logb, -1e30)
        cs = jnp.dot(l1mb.astype(BF16), tri, preferred_element_type=F32)
        w = jnp.exp(logb + cs[:, :tk] + jnp.concatenate([acc[r0:]] * (tk // LANES), axis=1))
        pv = jnp.dot(w.astype(BF16), v_ref[pl.ds(ks, tk), :], preferred_element_type=F32)
        if r0:
            pad = jnp.zeros((r0, LANES), F32)
            return acc + jnp.concatenate([pad, cs[:, tk:]], axis=0), out + jnp.concatenate([pad, pv], axis=0)
        return acc + cs[:, tk:], out + pv

    acc = jnp.zeros((tq, LANES), F32)
    out = jnp.zeros((tq, LANES), F32)
    for dd in reversed(range(nd)):
        acc, out = tile(qi * nd + dd, acc, out, dd * tk)

    def steps(j, n, carry):
        a, o = carry
        for u in range(n):
            a, o = tile(j - u, a, o, None)
        return a, o

    pairs = qi // 2
    acc, out = lax.fori_loop(
        0, pairs, lambda i, c: steps((qi - 2 * i) * nd - 1, 2 * nd, c), (acc, out))
    acc, out = lax.fori_loop(
        0, qi - 2 * pairs, lambda i, c: steps(nd - 1, nd, c), (acc, out))
    o_ref[...] = out.astype(o_ref.dtype)


def _sbattn(proj, tri, *, batch, seq, tq=SB_QBLOCK, tk=SB_KBLOCK):
    t = proj.shape[0]
    tq = min(tq, seq)
    nq = seq // tq
    kernel = functools.partial(_sb_kernel, tq=tq, tk=tk, scale=1.0 / math.sqrt(LANES))
    return pl.pallas_call(
        kernel,
        grid=(batch * SB_HEADS, nq),
        in_specs=[pl.BlockSpec((tq, LANES), lambda bh, qi: ((bh // SB_HEADS) * nq + qi, bh % SB_HEADS)),
                  pl.BlockSpec((seq, LANES), lambda bh, qi: (bh // SB_HEADS, SB_HEADS + bh % SB_HEADS)),
                  pl.BlockSpec((seq, LANES), lambda bh, qi: (bh // SB_HEADS, 2 * SB_HEADS + bh % SB_HEADS)),
                  pl.BlockSpec((tk, tk + LANES), lambda bh, qi: (0, 0))],
        out_specs=pl.BlockSpec((tq, LANES), lambda bh, qi: ((bh // SB_HEADS) * nq + qi, bh % SB_HEADS)),
        out_shape=jax.ShapeDtypeStruct((t, SB_HEADS * LANES), BF16),
        compiler_params=_params(("arbitrary", "arbitrary"), 40),
        name="sbattn",
    )(proj, proj, proj, tri)


def _hgrn_kernel(q_ref, f_ref, i_ref, g_ref, lb_ref, ng_ref, ltri_ref, ones_ref, *refs, rb, c, sb, nw):
    o_ref, st_ref = refs[nw], refs[2 * nw + 1]
    for w_ref, w16_ref in zip(refs[:nw], refs[nw + 1:2 * nw + 1]):
        w16_ref[...] = w_ref[...].astype(BF16)

    @pl.when(pl.program_id(1) == 0)
    def _():
        st_ref[...] = jnp.zeros_like(st_ref)

    lb = lb_ref[...]
    ng = ng_ref[...]
    ltri = ltri_ref[...]
    ones = ones_ref[...]
    lane8 = lax.broadcasted_iota(jnp.int32, (8, c), 1)
    sub8 = lax.broadcasted_iota(jnp.int32, (8, LANES), 0)

    pre = []
    for ci in range(rb // c):
        r0 = ci * c
        qf = q_ref[r0:r0 + c, :].astype(F32)
        ff = f_ref[r0:r0 + c, :].astype(F32)
        vb = i_ref[r0:r0 + c, :]
        fg = lb + (1.0 - lb) * jax.nn.sigmoid(ff)
        lf = jnp.log(fg)
        kk = 1.0 - fg
        hi = lf.astype(BF16)
        lo = (lf - hi.astype(F32)).astype(BF16)
        b = (jnp.dot(ltri, hi, preferred_element_type=F32)
             + jnp.dot(ltri, lo, preferred_element_type=F32))

        b2 = b * LOG2E

        groups = [jnp.zeros((8, c), F32), jnp.zeros((8, c), F32)]
        for i in range(1, c // sb):
            r = b2[i * sb - 1:i * sb, :]
            qs = (qf[i * sb:(i + 1) * sb, :] * jnp.exp2(b2[i * sb:(i + 1) * sb, :] - r)).astype(BF16)
            ks = (kk[:i * sb, :] * jnp.exp2(r - b2[:i * sb, :])).astype(BF16)
            ks = jnp.concatenate([ks, jnp.zeros((c - i * sb, LANES), BF16)], axis=0)
            off_diag = lax.dot_general(qs, ks, _NT, preferred_element_type=F32)
            groups.extend([off_diag[:8, :], off_diag[8:, :]])

        slabs = []
        for s in range(c):
            g0, end = (s // 8) * 8, (s // sb + 1) * sb
            p = qf[g0:end, :] * kk[s:s + 1, :] * jnp.exp2(b2[g0:end, :] - b2[s:s + 1, :])
            if s % 8:
                head = jnp.where(sub8 >= (s % 8), p[:8, :], 0.0)
                p = head if end - g0 == 8 else jnp.concatenate([head, p[8:, :]], axis=0)
            slabs.append(p)
        red = jnp.dot(jnp.concatenate(slabs, axis=0).astype(BF16), ones, preferred_element_type=F32)
        off = 0
        for s in range(c):
            for g in range(s // 8, (s // sb + 1) * (sb // 8)):
                groups[g] = jnp.where(lane8 == s, red[off:off + 8, :c], groups[g])
                off += 8
        scores = jnp.concatenate(groups, axis=0)

        o_intra = jnp.dot(scores.astype(BF16), vb, preferred_element_type=F32)
        b_last = b2[c - 1:c, :]
        qt = (qf * jnp.exp2(b2)).astype(BF16)
        kt = (kk * jnp.exp2(b_last - b2)).astype(BF16)
        upd = lax.dot_general(vb, kt, _TN, preferred_element_type=F32)
        pre.append((o_intra, qt, upd, jnp.exp2(b_last)))

    st = st_ref[...]
    for ci, (o_intra, qt, upd, dec) in enumerate(pre):
        r0 = ci * c
        o = o_intra + lax.dot_general(qt, st.astype(BF16), _NT, preferred_element_type=F32)
        st = st * dec + upd
        o = o * lax.rsqrt(jnp.mean(o * o, axis=-1, keepdims=True) + RMS_EPS)
        gg = g_ref[r0:r0 + c, :].astype(F32)
        o_ref[r0:r0 + c, :] = (o * ng * (gg * jax.nn.sigmoid(gg))).astype(o_ref.dtype)
    st_ref[...] = st


def _hgrn(proj, lb, ng, ltri, ones, weights, *, batch, seq, rb=4096):
    t = proj.shape[0]
    rb = min(rb, seq)
    nr = seq // rb
    nsteps = batch * HG_HEADS * nr
    base = 3 * SB_HEADS

    def col(k):
        return pl.BlockSpec((rb, LANES),
                            lambda bh, r: ((bh // HG_HEADS) * nr + r, base + k * HG_HEADS + bh % HG_HEADS))

    def slab(w):
        return pl.BlockSpec((w.shape[0] // nsteps, w.shape[1]), lambda bh, r: (bh * nr + r, 0))

    vec = pl.BlockSpec((None, 1, LANES), lambda bh, r: (bh % HG_HEADS, 0, 0))
    kernel = functools.partial(_hgrn_kernel, rb=rb, c=HG_CHUNK, sb=HG_SUB, nw=len(weights))
    return pl.pallas_call(
        kernel,
        grid=(batch * HG_HEADS, nr),
        in_specs=[col(0), col(1), col(2), col(3), vec, vec,
                  pl.BlockSpec((HG_CHUNK, HG_CHUNK), lambda bh, r: (0, 0)),
                  pl.BlockSpec((LANES, LANES), lambda bh, r: (0, 0)),
                  *[slab(w) for w in weights]],
        out_specs=[pl.BlockSpec((rb, LANES), lambda bh, r: ((bh // HG_HEADS) * nr + r, bh % HG_HEADS)),
                   *[slab(w) for w in weights]],
        out_shape=[jax.ShapeDtypeStruct((t, HG_HEADS * LANES), BF16),
                   *[jax.ShapeDtypeStruct(w.shape, BF16) for w in weights]],
        scratch_shapes=[pltpu.VMEM((LANES, LANES), F32)],
        compiler_params=_params(("arbitrary", "arbitrary"), 32),
        name="hgrn",
    )(proj, proj, proj, proj, lb, ng, ltri, ones, *weights)


def _layer_norm(z, g, b):
    mu = jnp.mean(z, axis=-1, keepdims=True)
    zc = z - mu
    var = jnp.mean(zc * zc, axis=-1, keepdims=True)
    return zc * lax.rsqrt(var + LN_EPS) * g + b


def _mix_kernel(a_ref, b_ref, ga0_ref, ga1_ref, gb0_ref, gb1_ref, x_ref, mod_ref, wa_ref, wb_ref,
                wo_ref, g_ref, be_ref, x1_ref, h2_ref, *, alpha):
    a = a_ref[...]
    b = b_ref[...]
    half = wa_ref.shape[1] // 2
    merged = []
    for n, (ga_ref, gb_ref) in enumerate(((ga0_ref, gb0_ref), (ga1_ref, gb1_ref))):
        pa = jnp.dot(a, wa_ref[:, n * half:(n + 1) * half], preferred_element_type=F32)
        pb = jnp.dot(b, wb_ref[:, n * half:(n + 1) * half], preferred_element_type=F32)
        m = (jax.nn.sigmoid(ga_ref[...].astype(F32)) * pa
             + jax.nn.sigmoid(gb_ref[...].astype(F32)) * pb)
        merged.append(m.astype(BF16))
    y = jnp.dot(jnp.concatenate(merged, axis=1), wo_ref[...], preferred_element_type=F32)
    gate1 = mod_ref[2:3, :]
    shift2 = mod_ref[3:4, :]
    scale2 = mod_ref[4:5, :]
    x1 = _layer_norm(alpha * x_ref[...] + (1.0 + gate1) * y, g_ref[...], be_ref[...])
    x1_ref[...] = x1
    h2_ref[...] = (x1 * (1.0 + scale2) + shift2).astype(BF16)


def _mix(a_out, b_out, proj, x2, mod, wa, wb, wo, ln_g, ln_b, *, seq, alpha, tm=512):
    t, d = x2.shape
    w = a_out.shape[1]
    tm = min(tm, seq)
    per_b = seq // tm
    gbase = (3 * SB_HEADS + 4 * HG_HEADS) * LANES // w
    row = lambda i: (i, 0)
    const = lambda i: (0, 0)

    def gate(k):
        return pl.BlockSpec((tm, w), lambda i: (i, gbase + k))

    kernel = functools.partial(_mix_kernel, alpha=alpha)
    return pl.pallas_call(
        kernel,
        grid=(t // tm,),
        in_specs=[pl.BlockSpec((tm, w), row), pl.BlockSpec((tm, w), row),
                  gate(0), gate(1), gate(2), gate(3),
                  pl.BlockSpec((tm, d), row),
                  pl.BlockSpec((None, 6, d), lambda i: (i // per_b, 0, 0)),
                  pl.BlockSpec((w, d), const, pipeline_mode=pl.Buffered(1)),
                  pl.BlockSpec((w, d), const, pipeline_mode=pl.Buffered(1)),
                  pl.BlockSpec((d, d), const, pipeline_mode=pl.Buffered(1)),
                  pl.BlockSpec((1, d), const), pl.BlockSpec((1, d), const)],
        out_specs=[pl.BlockSpec((tm, d), row), pl.BlockSpec((tm, d), row)],
        out_shape=[jax.ShapeDtypeStruct((t, d), F32), jax.ShapeDtypeStruct((t, d), BF16)],
        compiler_params=_params(("arbitrary",), 56),
        name="mix",
    )(a_out, b_out, proj, proj, proj, proj, x2, mod, wa, wb, wo, ln_g, ln_b)


def _top_rows(vals, payloads, k, n_rows):
    rid = lax.broadcasted_iota(jnp.int32, vals.shape, 0).astype(F32)
    best, rows, outs = [], [], [[] for _ in payloads]
    for _ in range(k):
        m = jnp.max(vals, axis=0, keepdims=True)
        first = jnp.min(jnp.where(vals == m, rid, float(n_rows)), axis=0, keepdims=True)
        sel = rid == first
        best.append(m)
        rows.append(first)
        for o, p in zip(outs, payloads):
            o.append(jnp.sum(jnp.where(sel, p, 0.0), axis=0, keepdims=True))
        vals = jnp.where(sel, -jnp.inf, vals)
    return best, rows, outs


def _route_kernel(h_ref, wq_ref, keys_ref, tab_ref, ee_ref, gt_ref, tab16_ref):
    tab16_ref[...] = tab_ref[...].astype(BF16)
    tb = h_ref.shape[0]
    k = PEER_TOPK
    q = jnp.dot(h_ref[...], wq_ref[...], preferred_element_type=F32).astype(BF16)
    ee, gt = [], []
    for h in range(PEER_HEADS):
        tops, idxs = [], []
        for p in range(2):
            c0 = (h * 2 + p) * LANES
            s = lax.dot_general(keys_ref[h * 2 + p], q[:, c0:c0 + LANES], _NT,
                                preferred_element_type=F32)
            best, rows, _ = _top_rows(s, [], k, N_KEYS)
            tops.append(best)
            idxs.append(rows)
        top1 = jnp.concatenate(tops[1], axis=0)
        idx1 = jnp.concatenate(idxs[1], axis=0)
        cand, ce = [], []
        for a in range(8):
            nb = k // (a + 1)
            cand.append(tops[0][a] + top1[:nb])
            ce.append(idxs[0][a] * float(N_KEYS) + idx1[:nb])
        npad = -sum(c.shape[0] for c in cand) % 8
        cand.append(jnp.full((npad, tb), -jnp.inf, F32))
        ce.append(jnp.zeros((npad, tb), F32))
        cand.append(jnp.concatenate(tops[0][8:], axis=0) + top1[:1])
        ce.append(jnp.concatenate(idxs[0][8:], axis=0) * float(N_KEYS) + idx1[:1])
        cand = jnp.concatenate(cand, axis=0)
        best, _, (be,) = _top_rows(cand, [jnp.concatenate(ce, axis=0)], k, cand.shape[0])
        e = [jnp.exp(v - best[0]) for v in best]
        denom = functools.reduce(lambda a, b: a + b, e)
        inv = 1.0 / denom
        ee.extend(be)
        gt.extend([v * inv for v in e])
    ee_ref[...] = jnp.concatenate(ee, axis=0).T
    gt_ref[...] = jnp.concatenate(gt, axis=0).T


def _route(h2, wq, keys, table, *, tb=256):
    t, d = h2.shape
    tb = min(tb, t)
    nk = PEER_HEADS * PEER_TOPK
    rows = table.shape[0] // (t // tb)
    out = jax.ShapeDtypeStruct((t, nk), F32)
    ospec = pl.BlockSpec((tb, nk), lambda i: (i, 0))
    tspec = pl.BlockSpec((rows, table.shape[1]), lambda i: (i, 0))
    return pl.pallas_call(
        _route_kernel,
        grid=(t // tb,),
        in_specs=[pl.BlockSpec((tb, d), lambda i: (i, 0)),
                  pl.BlockSpec(wq.shape, lambda i: (0, 0)),
                  pl.BlockSpec(keys.shape, lambda i: (0, 0, 0)),
                  tspec],
        out_specs=[ospec, ospec, tspec],
        out_shape=[out, out, jax.ShapeDtypeStruct(table.shape, BF16)],
        compiler_params=_params(("arbitrary",), 48),
        name="route",
    )(h2, wq, keys, table)


def _scatter_kernel(ee_ref, gt_ref, tab_ref, o_ref, tab16_ref, *, tb, grp):
    tab16_ref[...] = tab_ref[...].astype(BF16)
    sub = lax.broadcasted_iota(jnp.int32, (N_KEYS, LANES), 0).astype(F32).astype(BF16)
    one = jnp.ones((), BF16)
    zero = jnp.zeros((), BF16)

    def body(g, carry):
        t0 = pl.multiple_of(g * grp, grp)
        prods = []
        for u in range(grp):
            erow = ee_ref[pl.ds(t0 + u, 1), :]
            ihi = jnp.floor(erow * (1.0 / N_KEYS))
            irow = ihi.astype(BF16)
            jrow = (erow - ihi * float(N_KEYS)).astype(BF16)
            grow = gt_ref[pl.ds(t0 + u, 1), :].astype(BF16)
            r = jnp.where(sub == irow, grow, zero)
            cm = jnp.where(sub == jrow, one, zero)
            prods.append(lax.dot_general(r, cm, _NT, preferred_element_type=F32))
        y = jnp.swapaxes(jnp.stack(prods, axis=0), 0, 1).astype(o_ref.dtype)
        for i in range(N_KEYS):
            o_ref[pl.ds(t0, grp), i * N_KEYS:(i + 1) * N_KEYS] = y[i]
        return carry

    lax.fori_loop(0, tb // grp, body, 0)


def _scatter(ee, gt, table, *, tb=256, grp=16):
    t, nk = ee.shape
    tb = min(tb, t)
    rows = table.shape[0] // (t // tb)
    spec = pl.BlockSpec((tb, nk), lambda i: (i, 0))
    tspec = pl.BlockSpec((rows, table.shape[1]), lambda i: (i, 0))
    kernel = functools.partial(_scatter_kernel, tb=tb, grp=grp)
    return pl.pallas_call(
        kernel,
        grid=(t // tb,),
        in_specs=[spec, spec, tspec],
        out_specs=[pl.BlockSpec((tb, N_KEYS * N_KEYS), lambda i: (i, 0)), tspec],
        out_shape=[jax.ShapeDtypeStruct((t, N_KEYS * N_KEYS), BF16),
                   jax.ShapeDtypeStruct(table.shape, BF16)],
        compiler_params=_params(("arbitrary",), 48),
        name="scatter",
    )(ee, gt, table)


def _peer_kernel(h_ref, u_ref, v_ref, gm_ref, x1_ref, mod_ref, g_ref, be_ref, o_ref, *, alpha):
    j = pl.program_id(1)

    @pl.when(j == 0)
    def _():
        o_ref[...] = jnp.zeros_like(o_ref)

    act = lax.dot_general(h_ref[...], u_ref[...], _NT, preferred_element_type=F32)
    gelu = 0.5 * act * (1.0 + lax.erf(act * (1.0 / math.sqrt(2.0))))
    w = (gm_ref[...].astype(F32) * gelu).astype(BF16)
    o_ref[...] += jnp.dot(w, v_ref[...], preferred_element_type=F32)

    @pl.when(j == pl.num_programs(1) - 1)
    def _():
        gate2 = mod_ref[5:6, :]
        o_ref[...] = _layer_norm(alpha * x1_ref[...] + (1.0 + gate2) * o_ref[...],
                                 g_ref[...], be_ref[...])


def _peer(h2, u, v, gm, x1, mod, ln_g, ln_b, *, seq, alpha, tb=1024, eb=512, vmem_mib=62):
    t, d = h2.shape
    ne = v.shape[0]
    tb = min(tb, seq)
    per_b = seq // tb
    row = lambda i, j: (i, 0)
    const = lambda i, j: (0, 0)
    kernel = functools.partial(_peer_kernel, alpha=alpha)
    return pl.pallas_call(
        kernel,
        grid=(t // tb, ne // eb),
        in_specs=[pl.BlockSpec((tb, d), row),
                  pl.BlockSpec((eb, d), lambda i, j: (j, 0)),
                  pl.BlockSpec((eb, d), lambda i, j: (j, 0)),
                  pl.BlockSpec((tb, eb), lambda i, j: (i, j)),
                  pl.BlockSpec((tb, d), row),
                  pl.BlockSpec((None, 6, d), lambda i, j: (i // per_b, 0, 0)),
                  pl.BlockSpec((1, d), const), pl.BlockSpec((1, d), const)],
        out_specs=pl.BlockSpec((tb, d), row),
        out_shape=jax.ShapeDtypeStruct((t, d), F32),
        compiler_params=_params(("arbitrary", "arbitrary"), vmem_mib),
        name="peer",
    )(h2, u, v, gm, x1, mod, ln_g, ln_b)


def kernel(x, c, w_ada, b_ada, w_in, hg_lb_logits, hg_norm_g, w_branch_a, w_branch_b, w_out,
           ln1_g, ln1_b, peer_wq, peer_subkeys, peer_u, peer_v, ln2_g, ln2_b):
    batch, seq, d = x.shape
    depth = w_ada.shape[0]
    alpha = (2.0 * depth) ** 0.25
    lower_bounds = jnp.cumsum(jax.nn.softmax(hg_lb_logits.astype(F32), axis=0), axis=0)

    r = lax.broadcasted_iota(jnp.int32, (LANES, LANES), 0)
    cc = lax.broadcasted_iota(jnp.int32, (LANES, LANES), 1)
    rb = lax.broadcasted_iota(jnp.int32, (SB_KBLOCK, SB_KBLOCK), 0)
    cb = lax.broadcasted_iota(jnp.int32, (SB_KBLOCK, SB_KBLOCK), 1)
    tri = jnp.concatenate([(rb > cb), jnp.ones((SB_KBLOCK, LANES), bool)], axis=1).astype(BF16)
    ltri = (r[:HG_CHUNK, :HG_CHUNK] >= cc[:HG_CHUNK, :HG_CHUNK]).astype(BF16)
    ones = jnp.ones((LANES, LANES), BF16)

    c_pad = jnp.zeros((8, d), F32).at[:batch].set(c)
    xt = x.reshape(batch * seq, d)
    for layer in range(depth):
        mod = _ada(c_pad, w_ada[layer:layer + 1], b_ada[layer:layer + 1])[:batch].reshape(batch, 6, d)
        proj = _inproj(xt, mod, w_in[layer].astype(BF16), seq=seq)
        a_out = _sbattn(proj, tri, batch=batch, seq=seq)
        b_out, wo16, wq16, wa16, wb16 = _hgrn(
            proj, lower_bounds[layer].reshape(HG_HEADS, 1, LANES),
            hg_norm_g[layer].reshape(HG_HEADS, 1, LANES), ltri, ones,
            (w_out[layer], peer_wq[layer], w_branch_a[layer], w_branch_b[layer]), batch=batch, seq=seq)
        x1, h2 = _mix(a_out, b_out, proj, xt, mod, wa16, wb16, wo16,
                      ln1_g[layer:layer + 1], ln1_b[layer:layer + 1], seq=seq, alpha=alpha)
        keys = peer_subkeys[layer].reshape(PEER_HEADS * 2, N_KEYS, LANES).astype(BF16)
        ee, gt, u16 = _route(h2, wq16, keys, peer_u[layer])
        gm, v16 = _scatter(ee, gt, peer_v[layer])
        xt = _peer(h2, u16, v16, gm, x1, mod,
                   ln2_g[layer:layer + 1], ln2_b[layer:layer + 1], seq=seq, alpha=alpha)
    return xt.reshape(batch, seq, d)
```
